```python
import jax, jax.numpy as jnp
from jax import lax
import numpy as np

D_MODEL = 4096
BATCH = 4
SEQ = 2048
DEPTH = 1
DEC_BATCH = 16
DEC_SEQ = 64
PAST_LEN = 2048

CHUNK = 64
LEFT_CHUNKS = 8
BAND_WINDOW = LEFT_CHUNKS * CHUNK
HEAD_DIM = 128
ATT_WIDTH = D_MODEL // 2
CONV_WIDTH = D_MODEL - ATT_WIDTH
ATT_HEADS = ATT_WIDTH // HEAD_DIM
MIX_WIDTH = ATT_WIDTH + CONV_WIDTH
IN_WIDTH = 3 * ATT_WIDTH + 2 * CONV_WIDTH
MAX_REL = 256
CONV_K = 31
MEM_TOKENS = 256
MEM_HEADS = 4
MEM_HEAD_DIM = 128
MEM_WIDTH = MEM_HEADS * MEM_HEAD_DIM
N_EXPERTS = 32
TOP_K = 4
D_EXPERT = D_MODEL
SWIGLU_LIMIT = 7.0
SWIGLU_ALPHA = 1.702
MOE_BLOCK = 128
NORM_EPS = 1e-5
NEG_INF = -1e30

kernel_name = "chunk_band_conformer_hybrid_moe_step"


def rmsnorm(x, g):
    xf = x.astype(jnp.float32)
    y = xf * lax.rsqrt(jnp.mean(xf * xf, axis=-1, keepdims=True) + NORM_EPS)
    return (y * g.astype(jnp.float32)).astype(x.dtype)


def layernorm(x, g, b):
    xf = x.astype(jnp.float32)
    mu = jnp.mean(xf, axis=-1, keepdims=True)
    xc = xf - mu
    y = xc * lax.rsqrt(jnp.mean(xc * xc, axis=-1, keepdims=True) + NORM_EPS)
    return (y * g.astype(jnp.float32) + b.astype(jnp.float32)).astype(x.dtype)


def rel_bias(table, q_pos, k_pos):
    rel = jnp.clip(q_pos[:, None] - k_pos[None, :], -MAX_REL, MAX_REL) + MAX_REL
    return jnp.take(table, rel, axis=1).astype(jnp.float32)


def mixer_in(h, w_in):
    z = h @ w_in
    q, k, v, cv, cg = jnp.split(
        z, [ATT_WIDTH, 2 * ATT_WIDTH, 3 * ATT_WIDTH, 3 * ATT_WIDTH + CONV_WIDTH], axis=-1)
    shp = h.shape[:2] + (ATT_HEADS, HEAD_DIM)
    u = cv * jax.nn.sigmoid(cg)
    return q.reshape(shp), k.reshape(shp), v.reshape(shp), u


def band_attention_prompt(q, k, v, table):
    B, S = q.shape[:2]
    nc = S // CHUNK
    band = (LEFT_CHUNKS + 1) * CHUNK
    qc = q.reshape(B, nc, CHUNK, ATT_HEADS, HEAD_DIM)
    pad = ((0, 0), (BAND_WINDOW, 0), (0, 0), (0, 0))
    kpad = jnp.pad(k, pad).reshape(B, nc + LEFT_CHUNKS, CHUNK, ATT_HEADS, HEAD_DIM)
    vpad = jnp.pad(v, pad).reshape(B, nc + LEFT_CHUNKS, CHUNK, ATT_HEADS, HEAD_DIM)
    idx = jnp.arange(nc)[:, None] + jnp.arange(LEFT_CHUNKS + 1)[None, :]
    kb = kpad[:, idx].reshape(B, nc, band, ATT_HEADS, HEAD_DIM)
    vb = vpad[:, idx].reshape(B, nc, band, ATT_HEADS, HEAD_DIM)
    bias = rel_bias(table, BAND_WINDOW + jnp.arange(CHUNK), jnp.arange(band))
    abs_k = jnp.arange(nc)[:, None] * CHUNK - BAND_WINDOW + jnp.arange(band)[None, :]
    valid = abs_k >= 0
    s = jnp.einsum('bnqhd,bnkhd->bnhqk', qc, kb,
                   preferred_element_type=jnp.float32) * (HEAD_DIM ** -0.5) + bias
    s = jnp.where(valid[None, :, None, None, :], s, NEG_INF)
    p = jax.nn.softmax(s, axis=-1).astype(v.dtype)
    o = jnp.einsum('bnhqk,bnkhd->bnqhd', p, vb)
    return o.reshape(B, S, ATT_WIDTH)


def band_attention_sample(q, k, v, k_past, v_past, table):
    B, L = q.shape[:2]
    R = k_past.shape[1]
    kk = jnp.concatenate([k_past.astype(k.dtype), k], axis=1)
    vv = jnp.concatenate([v_past.astype(v.dtype), v], axis=1)
    bias = rel_bias(table, R + jnp.arange(L), jnp.arange(R + L))
    s = jnp.einsum('bqhd,bkhd->bhqk', q, kk,
                   preferred_element_type=jnp.float32) * (HEAD_DIM ** -0.5) + bias
    p = jax.nn.softmax(s, axis=-1).astype(v.dtype)
    o = jnp.einsum('bhqk,bkhd->bqhd', p, vv)
    return o.reshape(B, L, ATT_WIDTH)


def conv_branch(u, past, conv_w, conv_b, ln_g, ln_b):
    up = jnp.concatenate([past.astype(u.dtype), u], axis=1)
    y = lax.conv_general_dilated(
        up, conv_w.astype(u.dtype)[:, None, :], window_strides=(1,), padding='VALID',
        dimension_numbers=('NWC', 'WIO', 'NWC'), feature_group_count=CONV_WIDTH) + conv_b
    y = jax.nn.silu(layernorm(y, ln_g, ln_b))
    return y, up[:, -(CONV_K - 1):]


def merge_groups(att_o, conv_o, g_att, g_conv, w_out):
    z = jnp.concatenate([rmsnorm(att_o, g_att), rmsnorm(conv_o, g_conv)], axis=-1)
    return z @ w_out


def mem_kv(mem, g_mem, w_mk, w_mv):
    m = rmsnorm(mem, g_mem)
    shp = mem.shape[:2] + (MEM_HEADS, MEM_HEAD_DIM)
    return (m @ w_mk).reshape(shp), (m @ w_mv).reshape(shp)


def mem_attention(h, mk, mv, w_mq, w_mo):
    B, L = h.shape[:2]
    q = (h @ w_mq).reshape(B, L, MEM_HEADS, MEM_HEAD_DIM)
    s = jnp.einsum('bqhd,bkhd->bhqk', q, mk.astype(q.dtype),
                   preferred_element_type=jnp.float32) * (MEM_HEAD_DIM ** -0.5)
    p = jax.nn.softmax(s, axis=-1).astype(h.dtype)
    o = jnp.einsum('bhqk,bkhd->bqhd', p, mv.astype(h.dtype)).reshape(B, L, MEM_WIDTH)
    return o @ w_mo


def moe(h, w_router, b_router, w_gate_up, b_gate_up, w_down, b_down):
    B, L, D = h.shape
    xf = h.reshape(B * L, D)
    T = B * L
    logits = (xf @ w_router).astype(jnp.float32) + b_router.astype(jnp.float32)
    top_val, top_idx = lax.top_k(logits, TOP_K)
    gates = jax.nn.softmax(top_val, axis=-1)
    n_assign = T * TOP_K
    e_flat = top_idx.reshape(-1)
    g_flat = gates.reshape(-1)
    tok_flat = jnp.arange(n_assign, dtype=jnp.int32) // TOP_K
    order = jnp.argsort(e_flat)
    e_sorted = e_flat[order]
    counts = jnp.zeros((N_EXPERTS,), jnp.int32).at[e_flat].add(1)
    start = jnp.cumsum(counts) - counts
    padded = (counts + MOE_BLOCK - 1) // MOE_BLOCK * MOE_BLOCK
    pad_end = jnp.cumsum(padded)
    pad_start = pad_end - padded
    dest = pad_start[e_sorted] + jnp.arange(n_assign, dtype=jnp.int32) - start[e_sorted]
    n_blocks = -(-n_assign // MOE_BLOCK) + N_EXPERTS
    n_rows = n_blocks * MOE_BLOCK
    row_tok = jnp.zeros((n_rows,), jnp.int32).at[dest].set(tok_flat[order])
    row_gate = jnp.zeros((n_rows,), jnp.float32).at[dest].set(g_flat[order])
    blk_expert = jnp.minimum(
        jnp.searchsorted(pad_end, jnp.arange(n_blocks, dtype=jnp.int32) * MOE_BLOCK, side='right'),
        N_EXPERTS - 1)
    xs = xf[row_tok].reshape(n_blocks, MOE_BLOCK, D)

    def expert_block(args):
        xb, e = args
        gu = xb @ w_gate_up[e] + b_gate_up[e]
        gate = jnp.minimum(gu[:, :D_EXPERT], SWIGLU_LIMIT)
        up = jnp.clip(gu[:, D_EXPERT:], -SWIGLU_LIMIT, SWIGLU_LIMIT)
        act = (up + 1) * (gate * jax.nn.sigmoid(SWIGLU_ALPHA * gate))
        return act @ w_down[e] + b_down[e]

    ys = lax.map(expert_block, (xs, blk_expert)).reshape(n_rows, D)
    y = jax.ops.segment_sum(ys.astype(jnp.float32) * row_gate[:, None], row_tok, num_segments=T)
    return y.astype(h.dtype).reshape(B, L, D)


def setup_inputs(seed: int = 0) -> dict:
    key = jax.random.key(seed)
    ks = iter(jax.random.split(key, 48))
    nrm = lambda shape, scale: jax.random.normal(next(ks), shape, jnp.float32) * scale
    gain = lambda shape: 1.0 + nrm(shape, 0.01)
    R = min(BAND_WINDOW, PAST_LEN)
    L = DEPTH
    return {
        "x_prompt": nrm((BATCH, SEQ, D_MODEL), 1.0),
        "x_sample": nrm((DEC_BATCH, DEC_SEQ, D_MODEL), 1.0),
        "cache_band_k": nrm((L, DEC_BATCH, R, ATT_HEADS, HEAD_DIM), 1.0),
        "cache_band_v": nrm((L, DEC_BATCH, R, ATT_HEADS, HEAD_DIM), 1.0),
        "state_conv": nrm((L, DEC_BATCH, CONV_K - 1, CONV_WIDTH), 0.5),
        "cache_mem_k": nrm((L, DEC_BATCH, MEM_TOKENS, MEM_HEADS, MEM_HEAD_DIM), 1.0),
        "cache_mem_v": nrm((L, DEC_BATCH, MEM_TOKENS, MEM_HEADS, MEM_HEAD_DIM), 1.0),
        "mem_prompt": nrm((BATCH, MEM_TOKENS, D_MODEL), 1.0),
        "norm_mix": gain((L, D_MODEL)),
        "w_in": nrm((L, D_MODEL, IN_WIDTH), D_MODEL ** -0.5),
        "rel_table": nrm((L, ATT_HEADS, 2 * MAX_REL + 1), 0.1),
        "conv_w": nrm((L, CONV_K, CONV_WIDTH), CONV_K ** -0.5),
        "conv_b": nrm((L, CONV_WIDTH), 0.01),
        "conv_ln_g": gain((L, CONV_WIDTH)),
        "conv_ln_b": nrm((L, CONV_WIDTH), 0.01),
        "grp_g_att": gain((L, ATT_WIDTH)),
        "grp_g_conv": gain((L, CONV_WIDTH)),
        "w_out": nrm((L, MIX_WIDTH, D_MODEL), MIX_WIDTH ** -0.5),
        "norm_mem_q": gain((L, D_MODEL)),
        "norm_mem_kv": gain((L, D_MODEL)),
        "w_mem_q": nrm((L, D_MODEL, MEM_WIDTH), D_MODEL ** -0.5),
        "w_mem_k": nrm((L, D_MODEL, MEM_WIDTH), D_MODEL ** -0.5),
        "w_mem_v": nrm((L, D_MODEL, MEM_WIDTH), D_MODEL ** -0.5),
        "w_mem_o": nrm((L, MEM_WIDTH, D_MODEL), MEM_WIDTH ** -0.5),
        "norm_ffn": gain((L, D_MODEL)),
        "w_router": nrm((L, D_MODEL, N_EXPERTS), D_MODEL ** -0.5),
        "b_router": nrm((L, N_EXPERTS), 0.01),
        "w_gate_up": nrm((L, N_EXPERTS, D_MODEL, 2 * D_EXPERT), D_MODEL ** -0.5),
        "b_gate_up": nrm((L, N_EXPERTS, 2 * D_EXPERT), 0.01),
        "w_down": nrm((L, N_EXPERTS, D_EXPERT, D_MODEL), D_EXPERT ** -0.5),
        "b_down": nrm((L, N_EXPERTS, D_MODEL), 0.01),
        "norm_final": gain((D_MODEL,)),
    }


def reference(x_prompt, x_sample, cache_band_k, cache_band_v, state_conv, cache_mem_k,
              cache_mem_v, mem_prompt, norm_mix, w_in, rel_table, conv_w, conv_b, conv_ln_g,
              conv_ln_b, grp_g_att, grp_g_conv, w_out, norm_mem_q, norm_mem_kv, w_mem_q,
              w_mem_k, w_mem_v, w_mem_o, norm_ffn, w_router, b_router, w_gate_up, b_gate_up,
              w_down, b_down, norm_final):
    xp, xs = x_prompt, x_sample
    bk_p, bv_p, cs_p, mk_p, mv_p = [], [], [], [], []
    bk_s, bv_s, cs_s = [], [], []
    for l in range(DEPTH):
        hp = rmsnorm(xp, norm_mix[l])
        hs = rmsnorm(xs, norm_mix[l])
        qp, kp, vp, up = mixer_in(hp, w_in[l])
        qs, ks, vs, us = mixer_in(hs, w_in[l])
        att_p = band_attention_prompt(qp, kp, vp, rel_table[l])
        att_s = band_attention_sample(qs, ks, vs, cache_band_k[l], cache_band_v[l], rel_table[l])
        zero_past = jnp.zeros((xp.shape[0], CONV_K - 1, CONV_WIDTH), up.dtype)
        cnv_p, new_cs_p = conv_branch(up, zero_past, conv_w[l], conv_b[l], conv_ln_g[l], conv_ln_b[l])
        cnv_s, new_cs_s = conv_branch(us, state_conv[l], conv_w[l], conv_b[l], conv_ln_g[l], conv_ln_b[l])
        xp = xp + merge_groups(att_p, cnv_p, grp_g_att[l], grp_g_conv[l], w_out[l])
        xs = xs + merge_groups(att_s, cnv_s, grp_g_att[l], grp_g_conv[l], w_out[l])
        nb = min(BAND_WINDOW, xp.shape[1])
        bk_p.append(kp[:, -nb:])
        bv_p.append(vp[:, -nb:])
        cs_p.append(new_cs_p)
        bk_s.append(ks)
        bv_s.append(vs)
        cs_s.append(new_cs_s)
        mkp, mvp = mem_kv(mem_prompt, norm_mem_kv[l], w_mem_k[l], w_mem_v[l])
        mk_p.append(mkp)
        mv_p.append(mvp)
        xp = xp + mem_attention(rmsnorm(xp, norm_mem_q[l]), mkp, mvp, w_mem_q[l], w_mem_o[l])
        xs = xs + mem_attention(rmsnorm(xs, norm_mem_q[l]), cache_mem_k[l], cache_mem_v[l],
                                w_mem_q[l], w_mem_o[l])
        xp = xp + moe(rmsnorm(xp, norm_ffn[l]), w_router[l], b_router[l], w_gate_up[l],
                      b_gate_up[l], w_down[l], b_down[l])
        xs = xs + moe(rmsnorm(xs, norm_ffn[l]), w_router[l], b_router[l], w_gate_up[l],
                      b_gate_up[l], w_down[l], b_down[l])
    y_prompt = rmsnorm(xp, norm_final)
    y_sample = rmsnorm(xs, norm_final)
    return (y_prompt, y_sample, jnp.stack(bk_p), jnp.stack(bv_p), jnp.stack(cs_p),
            jnp.stack(mk_p), jnp.stack(mv_p), jnp.stack(bk_s), jnp.stack(bv_s), jnp.stack(cs_s))
```

```python
import functools

import jax
import jax.numpy as jnp
from jax import lax
from jax.experimental import pallas as pl
from jax.experimental.pallas import tpu as pltpu

F32 = jnp.float32
BF16 = jnp.bfloat16
I32 = jnp.int32

D_MODEL = 4096
CHUNK = 64
LEFT_CHUNKS = 8
BAND_WINDOW = LEFT_CHUNKS * CHUNK
HEAD_DIM = 128
ATT_WIDTH = 2048
CONV_WIDTH = 2048
ATT_HEADS = 16
IN_WIDTH = 3 * ATT_WIDTH + 2 * CONV_WIDTH
MAX_REL = 256
CONV_K = 31
MEM_TOKENS = 256
MEM_HEADS = 4
MEM_WIDTH = 512
N_EXPERTS = 32
TOP_K = 4
D_EXPERT = 4096
SWIGLU_LIMIT = 7.0
SWIGLU_ALPHA = 1.702
NORM_EPS = 1e-5
NEG_INF = -1e30

LANES = 128
VMEM_BYTES = 64 * 2**20

NORM_ROWS = 512
MM_ROWS = 1024
ATTN_HG = 8
QB = 256
KWIN = QB + BAND_WINDOW
ROLL_W = 1024
SKEYS = 640
CONV_TB = 256
CONV_HALO = 32
CONV_CW = 512
MOE_R = 256
MOE_NMAX = 1280
MOE_TN = 256
GATHER_ROWS = 256
COMBINE_TOK = 64


def _cparams(sem, vmem_mib):
    return pltpu.CompilerParams(dimension_semantics=sem, vmem_limit_bytes=vmem_mib * 2**20)


def _rmsnorm_kernel(x_ref, g_ref, o_ref):
    x = x_ref[...]
    ms = jnp.mean(x * x, axis=-1, keepdims=True)
    o_ref[...] = (x * lax.rsqrt(ms + NORM_EPS) * g_ref[...]).astype(o_ref.dtype)


def _rmsnorm(x, g, out_dtype):
    rows, d = x.shape
    tm = min(NORM_ROWS, rows)
    return pl.pallas_call(
        _rmsnorm_kernel,
        grid=(rows // tm,),
        in_specs=[pl.BlockSpec((tm, d), lambda i: (i, 0)),
                  pl.BlockSpec((1, d), lambda i: (0, 0))],
        out_specs=pl.BlockSpec((tm, d), lambda i: (i, 0)),
        out_shape=jax.ShapeDtypeStruct((rows, d), out_dtype),
        compiler_params=_cparams(("arbitrary",), 40),
        name="rmsnorm",
    )(x, g.reshape(1, d))


def _mm_kernel(*refs, nparts, has_res):
    a_refs = refs[:nparts]
    b_refs = refs[nparts:2 * nparts]
    o_ref = refs[-1]
    acc = jnp.dot(a_refs[0][...], b_refs[0][...], preferred_element_type=F32)
    for k in range(1, nparts):
        acc = acc + jnp.dot(a_refs[k][...], b_refs[k][...], preferred_element_type=F32)
    if has_res:
        acc = acc + refs[2 * nparts][...]
    o_ref[...] = acc.astype(o_ref.dtype)


def _matmul(a_parts, w, res, out_dtype, tn, name):
    m = a_parts[0].shape[0]
    kk = a_parts[0].shape[1]
    n = w.shape[1]
    nparts = len(a_parts)
    tm = min(MM_ROWS, m)
    in_specs = [pl.BlockSpec((tm, kk), lambda i, j: (i, 0)) for _ in a_parts]
    in_specs += [pl.BlockSpec((kk, tn), functools.partial(lambda i, j, p: (p, j), p=p))
                 for p in range(nparts)]
    args = list(a_parts) + [w] * nparts
    if res is not None:
        in_specs.append(pl.BlockSpec((tm, tn), lambda i, j: (i, j)))
        args.append(res)
    return pl.pallas_call(
        functools.partial(_mm_kernel, nparts=nparts, has_res=res is not None),
        grid=(m // tm, n // tn),
        in_specs=in_specs,
        out_specs=pl.BlockSpec((tm, tn), lambda i, j: (i, j)),
        out_shape=jax.ShapeDtypeStruct((m, n), out_dtype),
        compiler_params=_cparams(("arbitrary", "arbitrary"), 48),
        name=name,
    )(*args)


def _rel_rows(table):
    m = jnp.arange(ROLL_W)
    m = jnp.where(m < KWIN, m, m - ROLL_W)
    idx = jnp.clip(BAND_WINDOW - m, -MAX_REL, MAX_REL) + MAX_REL
    return table[:, idx].astype(F32)


def _toeplitz_bias(w_row, rows):
    wb = jnp.broadcast_to(w_row, (rows, ROLL_W))
    return pltpu.roll(wb, 0, 1, stride=1, stride_axis=0)


def _group_rmsnorm_store(o, g_ref, o_ref):
    ms = jnp.mean(o * o, axis=-1, keepdims=True)
    o_ref[...] = (o * lax.rsqrt(ms + NORM_EPS) * g_ref[...]).astype(o_ref.dtype)


def _attn_prompt_kernel(w_ref, q_ref, k0_ref, k1_ref, k2_ref, v0_ref, v1_ref, v2_ref, g_ref,
                        o_ref, bias_sc, o_sc):
    i = pl.program_id(1)
    hg = pl.program_id(2)
    first = (pl.program_id(0) == 0) & (i == 0) & (hg == 0)

    @pl.when(first)
    def _():
        r = lax.broadcasted_iota(I32, (QB, KWIN), 0) // CHUNK
        c = lax.broadcasted_iota(I32, (QB, KWIN), 1) // CHUNK
        band = (c >= r) & (c <= r + LEFT_CHUNKS)
        for h in range(ATT_HEADS):
            t = _toeplitz_bias(w_ref[h:h + 1, :], QB)
            bias_sc[h] = jnp.where(band, t[:, :KWIN], NEG_INF)

    kk = jnp.concatenate([r[...].astype(BF16) for r in (k0_ref, k1_ref, k2_ref)], axis=0)
    vv = jnp.concatenate([r[...].astype(BF16) for r in (v0_ref, v1_ref, v2_ref)], axis=0)
    col = lax.broadcasted_iota(I32, (QB, KWIN), 1)
    in_seq = col >= (BAND_WINDOW // QB - i) * QB
    scale = HEAD_DIM ** -0.5
    for hh in range(ATTN_HG):
        hs = slice(hh * HEAD_DIM, (hh + 1) * HEAD_DIM)
        qh = q_ref[:, hs].astype(BF16)
        s = lax.dot_general(qh, kk[:, hs], (((1,), (1,)), ((), ())),
                            preferred_element_type=F32) * scale + bias_sc[hg * ATTN_HG + hh]
        s = jnp.where(in_seq, s, NEG_INF)
        m = jnp.max(s, axis=-1, keepdims=True)
        p = jnp.exp(s - m)
        l = jnp.sum(p, axis=-1, keepdims=True)
        o = jnp.dot(p.astype(BF16), vv[:, hs], preferred_element_type=F32)
        o_sc[hg, :, hs] = o / l

    @pl.when(hg == ATT_HEADS // ATTN_HG - 1)
    def _():
        o = jnp.concatenate([o_sc[g] for g in range(ATT_HEADS // ATTN_HG)], axis=1)
        _group_rmsnorm_store(o, g_ref, o_ref)


def _attn_prompt(z, w_rows, g_att, batch, seq):
    nqb = seq // QB
    back = BAND_WINDOW // QB
    gw = ATTN_HG * HEAD_DIM
    ngrp = ATT_HEADS // ATTN_HG

    def kv_spec(col0, d):
        return pl.BlockSpec((QB, gw),
                            lambda b, i, g: (b * nqb + jnp.maximum(i - d, 0), col0 + g))

    kc, vc = ATT_WIDTH // gw, 2 * ATT_WIDTH // gw
    return pl.pallas_call(
        _attn_prompt_kernel,
        grid=(batch, nqb, ngrp),
        in_specs=[pl.BlockSpec((ATT_HEADS, ROLL_W), lambda b, i, g: (0, 0)),
                  pl.BlockSpec((QB, gw), lambda b, i, g: (b * nqb + i, g)),
                  kv_spec(kc, back), kv_spec(kc, back - 1), kv_spec(kc, 0),
                  kv_spec(vc, back), kv_spec(vc, back - 1), kv_spec(vc, 0),
                  pl.BlockSpec((1, ATT_WIDTH), lambda b, i, g: (0, 0))],
        out_specs=pl.BlockSpec((QB, ATT_WIDTH), lambda b, i, g: (b * nqb + i, 0)),
        out_shape=jax.ShapeDtypeStruct((batch * seq, ATT_WIDTH), BF16),
        scratch_shapes=[pltpu.VMEM((ATT_HEADS, QB, KWIN), F32),
                        pltpu.VMEM((ngrp, QB, gw), F32)],
        compiler_params=_cparams(("arbitrary", "arbitrary", "arbitrary"), 48),
        name="attn_prompt",
    )(w_rows, z, z, z, z, z, z, z, g_att.reshape(1, ATT_WIDTH))


def _attn_sample_kernel(w_ref, q_ref, kn_ref, vn_ref, kc_ref, vc_ref, g_ref, o_ref, bias_sc, o_sc,
                        *, past, new):
    @pl.when(pl.program_id(0) == 0)
    def _():
        c = lax.broadcasted_iota(I32, (new, SKEYS), 1)
        for h in range(ATT_HEADS):
            t = _toeplitz_bias(w_ref[h:h + 1, :], new)
            bias_sc[h] = jnp.where(c < past + new, t[:, :SKEYS], NEG_INF)

    pad = jnp.zeros((SKEYS - past - new, ATT_WIDTH), F32)
    kk = jnp.concatenate([kc_ref[0], kn_ref[...], pad], axis=0).astype(BF16)
    vv = jnp.concatenate([vc_ref[0], vn_ref[...], pad], axis=0).astype(BF16)
    scale = HEAD_DIM ** -0.5
    for h in range(ATT_HEADS):
        hs = slice(h * HEAD_DIM, (h + 1) * HEAD_DIM)
        qh = q_ref[:, hs].astype(BF16)
        s = lax.dot_general(qh, kk[:, hs], (((1,), (1,)), ((), ())),
                            preferred_element_type=F32) * scale + bias_sc[h]
        m = jnp.max(s, axis=-1, keepdims=True)
        p = jnp.exp(s - m)
        l = jnp.sum(p, axis=-1, keepdims=True)
        o = jnp.dot(p.astype(BF16), vv[:, hs], preferred_element_type=F32)
        o_sc[:, hs] = o / l
    _group_rmsnorm_store(o_sc[...], g_ref, o_ref)


def _attn_sample(z, row0, cache_k, cache_v, w_rows, g_att, batch, new):
    past = cache_k.shape[1]
    assert past == BAND_WINDOW and past + new <= SKEYS and row0 % new == 0
    blk0 = row0 // new
    return pl.pallas_call(
        functools.partial(_attn_sample_kernel, past=past, new=new),
        grid=(batch,),
        in_specs=[pl.BlockSpec((ATT_HEADS, ROLL_W), lambda b: (0, 0)),
                  pl.BlockSpec((new, ATT_WIDTH), lambda b: (blk0 + b, 0)),
                  pl.BlockSpec((new, ATT_WIDTH), lambda b: (blk0 + b, 1)),
                  pl.BlockSpec((new, ATT_WIDTH), lambda b: (blk0 + b, 2)),
                  pl.BlockSpec((1, past, ATT_WIDTH), lambda b: (b, 0, 0)),
                  pl.BlockSpec((1, past, ATT_WIDTH), lambda b: (b, 0, 0)),
                  pl.BlockSpec((1, ATT_WIDTH), lambda b: (0, 0))],
        out_specs=pl.BlockSpec((new, ATT_WIDTH), lambda b: (b, 0)),
        out_shape=jax.ShapeDtypeStruct((batch * new, ATT_WIDTH), BF16),
        scratch_shapes=[pltpu.VMEM((ATT_HEADS, new, SKEYS), F32),
                        pltpu.VMEM((new, ATT_WIDTH), F32)],
        compiler_params=_cparams(("arbitrary",), 48),
        name="attn_sample",
    )(w_rows, z, z, z, cache_k, cache_v, g_att.reshape(1, ATT_WIDTH))


def _conv_tail(u, hist, w_ref, cb_ref, lg_ref, lb_ref, gg_ref, o_ref, st_ref, u_sc, y_sc, tb):
    u_sc[0:CONV_HALO, :] = hist
    u_sc[CONV_HALO:CONV_HALO + tb, :] = u
    st_ref[0] = u[tb - CONV_HALO:, :]
    off = CONV_HALO - (CONV_K - 1)
    for cc in range(CONV_WIDTH // CONV_CW):
        cs = slice(cc * CONV_CW, (cc + 1) * CONV_CW)
        acc = u_sc[off:off + tb, cs] * w_ref[0:1, cs]
        for j in range(1, CONV_K):
            acc = acc + u_sc[off + j:off + j + tb, cs] * w_ref[j:j + 1, cs]
        y_sc[:, cs] = acc + cb_ref[:, cs]
    y = y_sc[...]
    mu = jnp.mean(y, axis=-1, keepdims=True)
    yc = y - mu
    y = yc * lax.rsqrt(jnp.mean(yc * yc, axis=-1, keepdims=True) + NORM_EPS)
    y = y * lg_ref[...] + lb_ref[...]
    y = y * jax.nn.sigmoid(y)
    _group_rmsnorm_store(y, gg_ref, o_ref)


def _conv_prompt_kernel(cv_ref, cg_ref, pv_ref, pg_ref, w_ref, cb_ref, lg_ref, lb_ref, gg_ref,
                        o_ref, st_ref, u_sc, y_sc):
    u = cv_ref[...] * jax.nn.sigmoid(cg_ref[...])
    hist = pv_ref[...] * jax.nn.sigmoid(pg_ref[...])
    hist = jnp.where(pl.program_id(1) > 0, hist, 0.0)
    _conv_tail(u, hist, w_ref, cb_ref, lg_ref, lb_ref, gg_ref, o_ref, st_ref, u_sc, y_sc, CONV_TB)


def _conv_sample_kernel(cv_ref, cg_ref, past_ref, w_ref, cb_ref, lg_ref, lb_ref, gg_ref,
                        o_ref, st_ref, u_sc, y_sc, *, tb):
    u = cv_ref[...] * jax.nn.sigmoid(cg_ref[...])
    _conv_tail(u, past_ref[0], w_ref, cb_ref, lg_ref, lb_ref, gg_ref, o_ref, st_ref, u_sc, y_sc, tb)


def _conv_param_specs(nidx):
    zero = (lambda b, t: (0, 0)) if nidx == 2 else (lambda b: (0, 0))
    return [pl.BlockSpec((CONV_K, CONV_WIDTH), zero)] + [pl.BlockSpec((1, CONV_WIDTH), zero)] * 4


def _conv_prompt(z, params, batch, seq):
    ntb = seq // CONV_TB
    per = CONV_TB // CONV_HALO
    cv_blk, cg_blk = 3 * ATT_WIDTH // CONV_WIDTH, 3 * ATT_WIDTH // CONV_WIDTH + 1

    def cur(col):
        return pl.BlockSpec((CONV_TB, CONV_WIDTH), lambda b, t: (b * ntb + t, col))

    def prev(col):
        return pl.BlockSpec((CONV_HALO, CONV_WIDTH),
                            lambda b, t: (jnp.maximum((b * ntb + t) * per - 1, 0), col))

    return pl.pallas_call(
        _conv_prompt_kernel,
        grid=(batch, ntb),
        in_specs=[cur(cv_blk), cur(cg_blk), prev(cv_blk), prev(cg_blk)] + _conv_param_specs(2),
        out_specs=[pl.BlockSpec((CONV_TB, CONV_WIDTH), lambda b, t: (b * ntb + t, 0)),
                   pl.BlockSpec((1, CONV_HALO, CONV_WIDTH), lambda b, t: (b, 0, 0))],
        out_shape=[jax.ShapeDtypeStruct((batch * seq, CONV_WIDTH), BF16),
                   jax.ShapeDtypeStruct((batch, CONV_HALO, CONV_WIDTH), F32)],
        scratch_shapes=[pltpu.VMEM((CONV_HALO + CONV_TB, CONV_WIDTH), F32),
                        pltpu.VMEM((CONV_TB, CONV_WIDTH), F32)],
        compiler_params=_cparams(("arbitrary", "arbitrary"), 48),
        name="conv_prompt",
    )(z, z, z, z, *params)


def _conv_sample(z, row0, past, params, batch, new):
    assert row0 % new == 0 and new >= CONV_HALO
    blk0 = row0 // new
    cv_blk, cg_blk = 3 * ATT_WIDTH // CONV_WIDTH, 3 * ATT_WIDTH // CONV_WIDTH + 1
    return pl.pallas_call(
        functools.partial(_conv_sample_kernel, tb=new),
        grid=(batch,),
        in_specs=[pl.BlockSpec((new, CONV_WIDTH), lambda b: (blk0 + b, cv_blk)),
                  pl.BlockSpec((new, CONV_WIDTH), lambda b: (blk0 + b, cg_blk)),
                  pl.BlockSpec((1, CONV_HALO, CONV_WIDTH), lambda b: (b, 0, 0))] + _conv_param_specs(1),
        out_specs=[pl.BlockSpec((new, CONV_WIDTH), lambda b: (b, 0)),
                   pl.BlockSpec((1, CONV_HALO, CONV_WIDTH), lambda b: (b, 0, 0))],
        out_shape=[jax.ShapeDtypeStruct((batch * new, CONV_WIDTH), BF16),
                   jax.ShapeDtypeStruct((batch, CONV_HALO, CONV_WIDTH), F32)],
        scratch_shapes=[pltpu.VMEM((CONV_HALO + new, CONV_WIDTH), F32),
                        pltpu.VMEM((new, CONV_WIDTH), F32)],
        compiler_params=_cparams(("arbitrary",), 32),
        name="conv_sample",
    )(z, z, past, *params)


def _memattn_kernel(q_ref, mk_ref, mv_ref, o_ref):
    mk = mk_ref[0].astype(BF16)
    mv = mv_ref[0].astype(BF16)
    scale = HEAD_DIM ** -0.5
    for h in range(MEM_HEADS):
        hs = slice(h * HEAD_DIM, (h + 1) * HEAD_DIM)
        qh = q_ref[:, hs].astype(BF16)
        s = lax.dot_general(qh, mk[:, hs], (((1,), (1,)), ((), ())),
                            preferred_element_type=F32) * scale
        m = jnp.max(s, axis=-1, keepdims=True)
        p = jnp.exp(s - m)
        l = jnp.sum(p, axis=-1, keepdims=True)
        o = jnp.dot(p.astype(BF16), mv[:, hs], preferred_element_type=F32)
        o_ref[:, hs] = (o / l).astype(o_ref.dtype)


def _memattn(q, row0, mk, mv, k_col, v_col, batch, rows_per_batch, tb, name):
    nb = rows_per_batch // tb
    blk0 = row0 // tb
    return pl.pallas_call(
        _memattn_kernel,
        grid=(batch, nb),
        in_specs=[pl.BlockSpec((tb, MEM_WIDTH), lambda b, t: (blk0 + b * nb + t, 0)),
                  pl.BlockSpec((1, MEM_TOKENS, MEM_WIDTH), lambda b, t: (b, 0, k_col)),
                  pl.BlockSpec((1, MEM_TOKENS, MEM_WIDTH), lambda b, t: (b, 0, v_col))],
        out_specs=pl.BlockSpec((tb, MEM_WIDTH), lambda b, t: (b * nb + t, 0)),
        out_shape=jax.ShapeDtypeStruct((batch * rows_per_batch, MEM_WIDTH), BF16),
        compiler_params=_cparams(("arbitrary", "arbitrary"), 32),
        name=name,
    )(q, mk, mv)


def _router_kernel(h_ref, w_ref, b_ref, idx_ref, gate_ref):
    logits = jnp.dot(h_ref[...], w_ref[...], preferred_element_type=F32,
                     precision=lax.Precision.HIGHEST) + b_ref[...]
    lane = lax.broadcasted_iota(I32, logits.shape, 1)
    logits = jnp.where(lane < N_EXPERTS, logits, -jnp.inf)
    vals, idxs = [], []
    for _ in range(TOP_K):
        m = jnp.max(logits, axis=-1, keepdims=True)
        idx = jnp.min(jnp.where(logits == m, lane, LANES), axis=-1, keepdims=True)
        vals.append(m)
        idxs.append(idx)
        logits = jnp.where(lane == idx, -jnp.inf, logits)
    es = [jnp.exp(v - vals[0]) for v in vals]
    tot = es[0]
    for e in es[1:]:
        tot = tot + e
    idx_out = jnp.zeros(logits.shape, I32)
    gate_out = jnp.zeros(logits.shape, F32)
    for k in range(TOP_K):
        idx_out = jnp.where(lane == k, idxs[k], idx_out)
        gate_out = jnp.where(lane == k, es[k] / tot, gate_out)
    idx_ref[...] = idx_out
    gate_ref[...] = gate_out


def _router(h, w_router, b_router):
    rows, d = h.shape
    tm = NORM_ROWS
    wp = jnp.zeros((d, LANES), F32).at[:, :N_EXPERTS].set(w_router)
    bp = jnp.zeros((1, LANES), F32).at[0, :N_EXPERTS].set(b_router)
    idx, gate = pl.pallas_call(
        _router_kernel,
        grid=(rows // tm,),
        in_specs=[pl.BlockSpec((tm, d), lambda i: (i, 0)),
                  pl.BlockSpec((d, LANES), lambda i: (0, 0)),
                  pl.BlockSpec((1, LANES), lambda i: (0, 0))],
        out_specs=[pl.BlockSpec((tm, LANES), lambda i: (i, 0)),
                   pl.BlockSpec((tm, LANES), lambda i: (i, 0))],
        out_shape=[jax.ShapeDtypeStruct((rows, LANES), I32),
                   jax.ShapeDtypeStruct((rows, LANES), F32)],
        compiler_params=_cparams(("arbitrary",), 40),
        name="router",
    )(h, wp, bp)
    return idx[:, :TOP_K], gate[:, :TOP_K]


def _moe_tiles(n_assign):
    s_max = n_assign // MOE_NMAX + N_EXPERTS
    return s_max, (s_max + 1) * MOE_NMAX


def _route(top_idx, gates):
    n = top_idx.size
    s_max, rows_tot = _moe_tiles(n)
    e_flat = top_idx.reshape(-1)
    order = jnp.argsort(e_flat, stable=True).astype(I32)
    e_sorted = e_flat[order]
    counts = jnp.zeros((N_EXPERTS,), I32).at[e_flat].add(1)
    start = jnp.cumsum(counts) - counts
    n_st = (counts + MOE_NMAX - 1) // MOE_NMAX
    st_end = jnp.cumsum(n_st)
    st_start = st_end - n_st
    n_live = st_end[-1]
    dest = st_start[e_sorted] * MOE_NMAX + jnp.arange(n, dtype=I32) - start[e_sorted]
    s_ids = jnp.arange(s_max, dtype=I32)
    live = s_ids < n_live
    st_e = jnp.minimum(jnp.searchsorted(st_end, s_ids, side="right"), N_EXPERTS - 1).astype(I32)
    st_rows = jnp.clip(counts[st_e] - (s_ids - st_start[st_e]) * MOE_NMAX, 0, MOE_NMAX)
    st_rows = jnp.where(live, st_rows, 0).astype(I32)
    st_e = jnp.where(live, st_e, st_e[jnp.maximum(n_live - 1, 0)])
    row_tok = jnp.zeros((rows_tot,), I32).at[dest].set(order // TOP_K)
    row_gate = jnp.zeros((rows_tot,), F32).at[dest].set(gates.reshape(-1)[order])
    pos = jnp.zeros((n,), I32).at[order].set(dest)
    per = MOE_NMAX // GATHER_ROWS
    blk = jnp.arange(rows_tot // GATHER_ROWS, dtype=I32)
    blk_rows = jnp.concatenate([st_rows, jnp.zeros((1,), I32)])[blk // per]
    blk_valid = jnp.clip(blk_rows - (blk % per) * GATHER_ROWS, 0, GATHER_ROWS).astype(I32)
    return dict(st_e=st_e, st_rows=st_rows, n_live=n_live.reshape(1).astype(I32), row_tok=row_tok,
                row_gate=row_gate.reshape(rows_tot, 1), pos=pos, blk_valid=blk_valid)


def _row_copy(src_hbm, src_row, dst, dst_row, sem):
    return pltpu.make_async_copy(src_hbm.at[pl.ds(src_row, 1), :], dst.at[pl.ds(dst_row, 1), :], sem)


def _gather_kernel(nv_ref, nl_ref, tok_ref, h_hbm, o_ref, buf, sem):
    i = pl.program_id(0)
    nv = nv_ref[i]

    @pl.when(i == 0)
    def _():
        buf[...] = jnp.zeros(buf.shape, buf.dtype)

    def issue(k, c):
        _row_copy(h_hbm, tok_ref[k], buf, k, sem).start()
        return c

    def wait(k, c):
        _row_copy(h_hbm, 0, buf, k, sem).wait()
        return c

    lax.fori_loop(0, nv, issue, 0)
    lax.fori_loop(0, nv, wait, 0)
    row = lax.broadcasted_iota(I32, (GATHER_ROWS, 1), 0)
    o_ref[...] = jnp.where(row < nv, buf[...], 0.0).astype(o_ref.dtype)


def _gather(h, route, rows_tot):
    d = h.shape[1]
    nblk = rows_tot // GATHER_ROWS
    dump = nblk - 1
    per = MOE_NMAX // GATHER_ROWS

    def out_map(i, nv, nl):
        return (jnp.where(i < nl[0] * per, i, dump), 0)

    return pl.pallas_call(
        _gather_kernel,
        grid_spec=pltpu.PrefetchScalarGridSpec(
            num_scalar_prefetch=2,
            grid=(nblk,),
            in_specs=[pl.BlockSpec((GATHER_ROWS,), lambda i, nv, nl: (i,), memory_space=pltpu.SMEM),
                      pl.BlockSpec(memory_space=pl.ANY)],
            out_specs=pl.BlockSpec((GATHER_ROWS, d), out_map),
            scratch_shapes=[pltpu.VMEM((GATHER_ROWS, d), F32), pltpu.SemaphoreType.DMA]),
        out_shape=jax.ShapeDtypeStruct((rows_tot, d), BF16),
        compiler_params=_cparams(("arbitrary",), 32),
        name="moe_gather",
    )(route["blk_valid"], route["n_live"], route["row_tok"], h)


def _row_passes(cnt):
    return (cnt + MOE_R - 1) // MOE_R


def _zero_tail(o_ref, npass):
    def zbody(r, c):
        rows = pl.ds(pl.multiple_of(r * MOE_R, MOE_R), MOE_R)
        o_ref[rows, :] = jnp.zeros((MOE_R, o_ref.shape[1]), o_ref.dtype)
        return c
    lax.fori_loop(npass, MOE_NMAX // MOE_R, zbody, 0)


def _moe_up_kernel(st_e, st_n, nl, xs_ref, wg_ref, wu_ref, bg_ref, bu_ref, o_ref, wgb, wub):
    cnt = st_n[pl.program_id(0)]
    npass = _row_passes(cnt)

    @pl.when(cnt > 0)
    def _():
        wgb[...] = wg_ref[0].astype(BF16)
        wub[...] = wu_ref[0].astype(BF16)

    def body(r, c):
        rows = pl.ds(pl.multiple_of(r * MOE_R, MOE_R), MOE_R)
        x = xs_ref[rows, :]
        g = jnp.dot(x, wgb[...], preferred_element_type=F32) + bg_ref[0]
        u = jnp.dot(x, wub[...], preferred_element_type=F32) + bu_ref[0]
        g = jnp.minimum(g, SWIGLU_LIMIT)
        u = jnp.clip(u, -SWIGLU_LIMIT, SWIGLU_LIMIT)
        o_ref[rows, :] = ((u + 1.0) * (g * jax.nn.sigmoid(SWIGLU_ALPHA * g))).astype(o_ref.dtype)
        return c

    lax.fori_loop(0, npass, body, 0)
    _zero_tail(o_ref, npass)


def _moe_down_kernel(st_e, st_n, nl, a_ref, wd_ref, bd_ref, gate_ref, o_ref, wdb):
    cnt = st_n[pl.program_id(0)]
    npass = _row_passes(cnt)

    @pl.when(cnt > 0)
    def _():
        wdb[...] = wd_ref[0].astype(BF16)

    def body(r, c):
        rows = pl.ds(pl.multiple_of(r * MOE_R, MOE_R), MOE_R)
        y = jnp.dot(a_ref[rows, :], wdb[...], preferred_element_type=F32) + bd_ref[0]
        o_ref[rows, :] = y * gate_ref[rows, :]
        return c

    lax.fori_loop(0, npass, body, 0)
    _zero_tail(o_ref, npass)


def _moe_index_maps(s_max, n_col_steps):
    def live(s, nl):
        return s < nl[0]

    def rows_in(s, j, st_e, st_n, nl):
        return (jnp.minimum(s, nl[0] - 1), 0)

    def w_cols(off):
        def f(s, j, st_e, st_n, nl):
            return (st_e[s], 0, off + jnp.where(live(s, nl), j, n_col_steps - 1))
        return f

    def rows_out(s, j, st_e, st_n, nl):
        return (jnp.where(live(s, nl), s, s_max), jnp.where(live(s, nl), j, 0))

    return rows_in, w_cols, rows_out


def _moe_up(xs, route, w_gate_up, b_gate_up, s_max):
    rows_tot, d = xs.shape
    nj = D_EXPERT // MOE_TN
    rows_in, w_cols, rows_out = _moe_index_maps(s_max, nj)
    return pl.pallas_call(
        _moe_up_kernel,
        grid_spec=pltpu.PrefetchScalarGridSpec(
            num_scalar_prefetch=3,
            grid=(s_max, nj),
            in_specs=[pl.BlockSpec((MOE_NMAX, d), rows_in),
                      pl.BlockSpec((1, d, MOE_TN), w_cols(0)),
                      pl.BlockSpec((1, d, MOE_TN), w_cols(nj)),
                      pl.BlockSpec((1, 1, MOE_TN), w_cols(0)),
                      pl.BlockSpec((1, 1, MOE_TN), w_cols(nj))],
            out_specs=pl.BlockSpec((MOE_NMAX, MOE_TN), rows_out),
            scratch_shapes=[pltpu.VMEM((d, MOE_TN), BF16), pltpu.VMEM((d, MOE_TN), BF16)]),
        out_shape=jax.ShapeDtypeStruct((rows_tot, D_EXPERT), BF16),
        compiler_params=_cparams(("arbitrary", "arbitrary"), 52),
        name="moe_up",
    )(route["st_e"], route["st_rows"], route["n_live"], xs, w_gate_up, w_gate_up,
      b_gate_up.reshape(N_EXPERTS, 1, 2 * D_EXPERT), b_gate_up.reshape(N_EXPERTS, 1, 2 * D_EXPERT))


def _moe_down(act, route, w_down, b_down, s_max):
    rows_tot, f = act.shape
    d = w_down.shape[2]
    nj = d // MOE_TN
    rows_in, w_cols, rows_out = _moe_index_maps(s_max, nj)
    return pl.pallas_call(
        _moe_down_kernel,
        grid_spec=pltpu.PrefetchScalarGridSpec(
            num_scalar_prefetch=3,
            grid=(s_max, nj),
            in_specs=[pl.BlockSpec((MOE_NMAX, f), rows_in),
                      pl.BlockSpec((1, f, MOE_TN), w_cols(0)),
                      pl.BlockSpec((1, 1, MOE_TN), w_cols(0)),
                      pl.BlockSpec((MOE_NMAX, 1), rows_in)],
            out_specs=pl.BlockSpec((MOE_NMAX, MOE_TN), rows_out),
            scratch_shapes=[pltpu.VMEM((f, MOE_TN), BF16)]),
        out_shape=jax.ShapeDtypeStruct((rows_tot, d), F32),
        compiler_params=_cparams(("arbitrary", "arbitrary"), 48),
        name="moe_down",
    )(route["st_e"], route["st_rows"], route["n_live"], act, w_down,
      b_down.reshape(N_EXPERTS, 1, d), route["row_gate"])


def _combine_kernel(pos_ref, ys_hbm, x_ref, g_ref, o_ref, buf, sem):
    n = COMBINE_TOK * TOP_K

    def issue(a, c):
        pltpu.make_async_copy(ys_hbm.at[pl.ds(pos_ref[a], 1), :],
                              buf.at[a % TOP_K, pl.ds(a // TOP_K, 1), :], sem).start()
        return c

    def wait(a, c):
        pltpu.make_async_copy(ys_hbm.at[pl.ds(0, 1), :],
                              buf.at[a % TOP_K, pl.ds(a // TOP_K, 1), :], sem).wait()
        return c

    lax.fori_loop(0, n, issue, 0)
    lax.fori_loop(0, n, wait, 0)
    y = buf[0]
    for k in range(1, TOP_K):
        y = y + buf[k]
    x = x_ref[...] + y
    ms = jnp.mean(x * x, axis=-1, keepdims=True)
    o_ref[...] = x * lax.rsqrt(ms + NORM_EPS) * g_ref[...]


def _combine(ys, pos, x, g_final):
    rows, d = x.shape
    return pl.pallas_call(
        _combine_kernel,
        grid_spec=pltpu.PrefetchScalarGridSpec(
            num_scalar_prefetch=0,
            grid=(rows // COMBINE_TOK,),
            in_specs=[pl.BlockSpec((COMBINE_TOK * TOP_K,), lambda i: (i,), memory_space=pltpu.SMEM),
                      pl.BlockSpec(memory_space=pl.ANY),
                      pl.BlockSpec((COMBINE_TOK, d), lambda i: (i, 0)),
                      pl.BlockSpec((1, d), lambda i: (0, 0))],
            out_specs=pl.BlockSpec((COMBINE_TOK, d), lambda i: (i, 0)),
            scratch_shapes=[pltpu.VMEM((TOP_K, COMBINE_TOK, d), F32), pltpu.SemaphoreType.DMA]),
        out_shape=jax.ShapeDtypeStruct((rows, d), F32),
        compiler_params=_cparams(("arbitrary",), 32),
        name="moe_combine",
    )(pos, ys, x, g_final.reshape(1, d))


def kernel(x_prompt, x_sample, cache_band_k, cache_band_v, state_conv, cache_mem_k, cache_mem_v, mem_prompt, norm_mix, w_in, rel_table, conv_w, conv_b, conv_ln_g, conv_ln_b, grp_g_att, grp_g_conv, w_out, norm_mem_q, norm_mem_kv, w_mem_q, w_mem_k, w_mem_v, w_mem_o, norm_ffn, w_router, b_router, w_gate_up, b_gate_up, w_down, b_down, norm_final):
    depth = norm_mix.shape[0]
    assert depth == 1
    bp, sp, d = x_prompt.shape
    bs, ss, _ = x_sample.shape
    tp, ts = bp * sp, bs * ss
    t_all = tp + ts
    assert d == D_MODEL and sp % QB == 0 and sp % CONV_TB == 0 and tp % MM_ROWS == 0
    assert sp >= BAND_WINDOW and ss == CHUNK and t_all % MM_ROWS == 0
    l = 0

    x = jnp.concatenate([x_prompt.reshape(tp, d), x_sample.reshape(ts, d)], axis=0)

    h = _rmsnorm(x, norm_mix[l], BF16)
    z = _matmul([h], w_in[l].astype(BF16), None, F32, 1024, "in_proj")
    w_rows = _rel_rows(rel_table[l])
    att_p = _attn_prompt(z, w_rows, grp_g_att[l], bp, sp)
    ck = cache_band_k[l].reshape(bs, -1, ATT_WIDTH)
    cv = cache_band_v[l].reshape(bs, -1, ATT_WIDTH)
    att_s = _attn_sample(z, tp, ck, cv, w_rows, grp_g_att[l], bs, ss)
    conv_params = (conv_w[l], conv_b[l].reshape(1, -1), conv_ln_g[l].reshape(1, -1),
                   conv_ln_b[l].reshape(1, -1), grp_g_conv[l].reshape(1, -1))
    cnv_p, cs_p = _conv_prompt(z, conv_params, bp, sp)
    past = jnp.pad(state_conv[l], ((0, 0), (CONV_HALO - (CONV_K - 1), 0), (0, 0)))
    cnv_s, cs_s = _conv_sample(z, tp, past, conv_params, bs, ss)
    att = jnp.concatenate([att_p, att_s], axis=0)
    cnv = jnp.concatenate([cnv_p, cnv_s], axis=0)
    x1 = _matmul([att, cnv], w_out[l].astype(BF16), x, F32, 512, "out_proj")

    zp = z[:tp].reshape(bp, sp, IN_WIDTH)
    zs = z[tp:].reshape(bs, ss, IN_WIDTH)
    hd = (ATT_HEADS, HEAD_DIM)
    bk_p = zp[:, sp - BAND_WINDOW:, ATT_WIDTH:2 * ATT_WIDTH].reshape(1, bp, BAND_WINDOW, *hd)
    bv_p = zp[:, sp - BAND_WINDOW:, 2 * ATT_WIDTH:3 * ATT_WIDTH].reshape(1, bp, BAND_WINDOW, *hd)
    bk_s = zs[:, :, ATT_WIDTH:2 * ATT_WIDTH].reshape(1, bs, ss, *hd)
    bv_s = zs[:, :, 2 * ATT_WIDTH:3 * ATT_WIDTH].reshape(1, bs, ss, *hd)
    keep = CONV_HALO - (CONV_K - 1)
    cs_p = cs_p[:, keep:][None]
    cs_s = cs_s[:, keep:][None]

    nm = mem_prompt.shape[1]
    mem_n = _rmsnorm(mem_prompt.reshape(bp * nm, d), norm_mem_kv[l], BF16)
    w_mkv = jnp.concatenate([w_mem_k[l], w_mem_v[l]], axis=1).astype(BF16)
    mkv = _matmul([mem_n], w_mkv, None, F32, 2 * MEM_WIDTH, "mem_kv").reshape(bp, nm, 2 * MEM_WIDTH)
    mk_p = mkv[:, :, :MEM_WIDTH].reshape(1, bp, nm, MEM_HEADS, HEAD_DIM)
    mv_p = mkv[:, :, MEM_WIDTH:].reshape(1, bp, nm, MEM_HEADS, HEAD_DIM)
    hq = _rmsnorm(x1, norm_mem_q[l], BF16)
    q = _matmul([hq], w_mem_q[l].astype(BF16), None, F32, MEM_WIDTH, "mem_q")
    mo_p = _memattn(q, 0, mkv, mkv, 0, 1, bp, sp, 512, "memattn_prompt")
    cmk = cache_mem_k[l].reshape(bs, nm, MEM_WIDTH)
    cmv = cache_mem_v[l].reshape(bs, nm, MEM_WIDTH)
    mo_s = _memattn(q, tp, cmk, cmv, 0, 0, bs, ss, ss, "memattn_sample")
    mo = jnp.concatenate([mo_p, mo_s], axis=0)
    x2 = _matmul([mo], w_mem_o[l].astype(BF16), x1, F32, 512, "mem_o")

    hf = _rmsnorm(x2, norm_ffn[l], F32)
    top_idx, gates = _router(hf, w_router[l], b_router[l])
    route = _route(top_idx, gates)
    s_max, rows_tot = _moe_tiles(t_all * TOP_K)
    xs = _gather(hf, route, rows_tot)
    act = _moe_up(xs, route, w_gate_up[l], b_gate_up[l], s_max)
    ys = _moe_down(act, route, w_down[l], b_down[l], s_max)
    y = _combine(ys, route["pos"], x2, norm_final)

    y_prompt = y[:tp].reshape(bp, sp, d)
    y_sample = y[tp:].reshape(bs, ss, d)
    return (y_prompt, y_sample, bk_p, bv_p, cs_p, mk_p, mv_p, bk_s, bv_s, cs_s)
```

```python
import functools

import jax
import jax.numpy as jnp
from jax import lax
from jax.experimental import pallas as pl
from jax.experimental.pallas import tpu as pltpu

F32 = jnp.float32
BF16 = jnp.bfloat16
I32 = jnp.int32

D_MODEL = 4096
CHUNK = 64
LEFT_CHUNKS = 8
BAND_WINDOW = LEFT_CHUNKS * CHUNK
HEAD_DIM = 128
ATT_WIDTH = 2048
CONV_WIDTH = 2048
ATT_HEADS = 16
IN_WIDTH = 3 * ATT_WIDTH + 2 * CONV_WIDTH
MAX_REL = 256
CONV_K = 31
MEM_TOKENS = 256
MEM_HEADS = 4
MEM_WIDTH = 512
N_EXPERTS = 32
TOP_K = 4
D_EXPERT = 4096
SWIGLU_LIMIT = 7.0
SWIGLU_ALPHA = 1.702
NORM_EPS = 1e-5
NEG_INF = -1e30

LANES = 128

NORM_ROWS = 512
MM_ROWS = 1024
ATTN_HG = 8
QB = 256
KWIN = QB + BAND_WINDOW
ROLL_W = 1024
SKEYS = 640
CONV_TB = 256
CONV_HALO = 32
CONV_CW = 512
MOE_R = 256
MOE_NMAX = 1280
MOE_TN_UP = 256
MOE_TN_DOWN = 512
GATHER_ROWS = MOE_R
COMBINE_TOK = 64
DMA_UNROLL = 8


def _cparams(sem, vmem_mib):
    return pltpu.CompilerParams(dimension_semantics=sem, vmem_limit_bytes=vmem_mib * 2**20)


def _rmsnorm_kernel(x_ref, g_ref, o_ref):
    x = x_ref[...]
    ms = jnp.mean(x * x, axis=-1, keepdims=True)
    o_ref[...] = (x * lax.rsqrt(ms + NORM_EPS) * g_ref[...]).astype(o_ref.dtype)


def _rmsnorm(x, g, out_dtype):
    rows, d = x.shape
    tm = min(NORM_ROWS, rows)
    return pl.pallas_call(
        _rmsnorm_kernel,
        grid=(rows // tm,),
        in_specs=[pl.BlockSpec((tm, d), lambda i: (i, 0)),
                  pl.BlockSpec((1, d), lambda i: (0, 0))],
        out_specs=pl.BlockSpec((tm, d), lambda i: (i, 0)),
        out_shape=jax.ShapeDtypeStruct((rows, d), out_dtype),
        compiler_params=_cparams(("arbitrary",), 40),
        name="rmsnorm",
    )(x, g.reshape(1, d))


def _mm_kernel(*refs, nparts, has_res):
    a_refs = refs[:nparts]
    b_refs = refs[nparts:2 * nparts]
    o_ref = refs[-1]
    acc = jnp.dot(a_refs[0][...], b_refs[0][...], preferred_element_type=F32)
    for k in range(1, nparts):
        acc = acc + jnp.dot(a_refs[k][...], b_refs[k][...], preferred_element_type=F32)
    if has_res:
        acc = acc + refs[2 * nparts][...]
    o_ref[...] = acc.astype(o_ref.dtype)


def _matmul(a_parts, w, res, out_dtype, tn, name):
    m = a_parts[0].shape[0]
    kk = a_parts[0].shape[1]
    n = w.shape[1]
    nparts = len(a_parts)
    tm = min(MM_ROWS, m)
    in_specs = [pl.BlockSpec((tm, kk), lambda i, j: (i, 0)) for _ in a_parts]
    in_specs += [pl.BlockSpec((kk, tn), functools.partial(lambda i, j, p: (p, j), p=p))
                 for p in range(nparts)]
    args = list(a_parts) + [w] * nparts
    if res is not None:
        in_specs.append(pl.BlockSpec((tm, tn), lambda i, j: (i, j)))
        args.append(res)
    return pl.pallas_call(
        functools.partial(_mm_kernel, nparts=nparts, has_res=res is not None),
        grid=(m // tm, n // tn),
        in_specs=in_specs,
        out_specs=pl.BlockSpec((tm, tn), lambda i, j: (i, j)),
        out_shape=jax.ShapeDtypeStruct((m, n), out_dtype),
        compiler_params=_cparams(("arbitrary", "arbitrary"), 48),
        name=name,
    )(*args)


def _rel_rows(table):
    m = jnp.arange(ROLL_W)
    m = jnp.where(m < KWIN, m, m - ROLL_W)
    idx = jnp.clip(BAND_WINDOW - m, -MAX_REL, MAX_REL) + MAX_REL
    return table[:, idx].astype(F32)


def _toeplitz_bias(w_row, rows):
    wb = jnp.broadcast_to(w_row, (rows, ROLL_W))
    return pltpu.roll(wb, 0, 1, stride=1, stride_axis=0)


def _group_rmsnorm_store(o, g_ref, o_ref):
    ms = jnp.mean(o * o, axis=-1, keepdims=True)
    o_ref[...] = (o * lax.rsqrt(ms + NORM_EPS) * g_ref[...]).astype(o_ref.dtype)


def _attn_prompt_kernel(w_ref, q_ref, k0_ref, k1_ref, k2_ref, v0_ref, v1_ref, v2_ref, g_ref,
                        o_ref, bias_sc, o_sc):
    i = pl.program_id(1)
    hg = pl.program_id(2)
    first = (pl.program_id(0) == 0) & (i == 0) & (hg == 0)

    @pl.when(first)
    def _():
        r = lax.broadcasted_iota(I32, (QB, KWIN), 0) // CHUNK
        c = lax.broadcasted_iota(I32, (QB, KWIN), 1) // CHUNK
        band = (c >= r) & (c <= r + LEFT_CHUNKS)
        for h in range(ATT_HEADS):
            t = _toeplitz_bias(w_ref[h:h + 1, :], QB)
            bias_sc[h] = jnp.where(band, t[:, :KWIN], NEG_INF)

    kk = jnp.concatenate([r[...].astype(BF16) for r in (k0_ref, k1_ref, k2_ref)], axis=0)
    vv = jnp.concatenate([r[...].astype(BF16) for r in (v0_ref, v1_ref, v2_ref)], axis=0)
    col = lax.broadcasted_iota(I32, (QB, KWIN), 1)
    in_seq = col >= (BAND_WINDOW // QB - i) * QB
    scale = HEAD_DIM ** -0.5
    for hh in range(ATTN_HG):
        hs = slice(hh * HEAD_DIM, (hh + 1) * HEAD_DIM)
        qh = q_ref[:, hs].astype(BF16)
        s = lax.dot_general(qh, kk[:, hs], (((1,), (1,)), ((), ())),
                            preferred_element_type=F32) * scale + bias_sc[hg * ATTN_HG + hh]
        s = jnp.where(in_seq, s, NEG_INF)
        m = jnp.max(s, axis=-1, keepdims=True)
        p = jnp.exp(s - m)
        l = jnp.sum(p, axis=-1, keepdims=True)
        o = jnp.dot(p.astype(BF16), vv[:, hs], preferred_element_type=F32)
        o_sc[hg, :, hs] = o / l

    @pl.when(hg == ATT_HEADS // ATTN_HG - 1)
    def _():
        o = jnp.concatenate([o_sc[g] for g in range(ATT_HEADS // ATTN_HG)], axis=1)
        _group_rmsnorm_store(o, g_ref, o_ref)


def _attn_prompt(z, w_rows, g_att, batch, seq):
    nqb = seq // QB
    back = BAND_WINDOW // QB
    gw = ATTN_HG * HEAD_DIM
    ngrp = ATT_HEADS // ATTN_HG

    def kv_spec(col0, d):
        return pl.BlockSpec((QB, gw),
                            lambda b, i, g: (b * nqb + jnp.maximum(i - d, 0), col0 + g))

    kc, vc = ATT_WIDTH // gw, 2 * ATT_WIDTH // gw
    return pl.pallas_call(
        _attn_prompt_kernel,
        grid=(batch, nqb, ngrp),
        in_specs=[pl.BlockSpec((ATT_HEADS, ROLL_W), lambda b, i, g: (0, 0)),
                  pl.BlockSpec((QB, gw), lambda b, i, g: (b * nqb + i, g)),
                  kv_spec(kc, back), kv_spec(kc, back - 1), kv_spec(kc, 0),
                  kv_spec(vc, back), kv_spec(vc, back - 1), kv_spec(vc, 0),
                  pl.BlockSpec((1, ATT_WIDTH), lambda b, i, g: (0, 0))],
        out_specs=pl.BlockSpec((QB, ATT_WIDTH), lambda b, i, g: (b * nqb + i, 0)),
        out_shape=jax.ShapeDtypeStruct((batch * seq, ATT_WIDTH), BF16),
        scratch_shapes=[pltpu.VMEM((ATT_HEADS, QB, KWIN), F32),
                        pltpu.VMEM((ngrp, QB, gw), F32)],
        compiler_params=_cparams(("arbitrary", "arbitrary", "arbitrary"), 48),
        name="attn_prompt",
    )(w_rows, z, z, z, z, z, z, z, g_att.reshape(1, ATT_WIDTH))


def _attn_sample_kernel(w_ref, q_ref, kn_ref, vn_ref, kc_ref, vc_ref, g_ref, o_ref, bias_sc, o_sc,
                        *, past, new):
    @pl.when(pl.program_id(0) == 0)
    def _():
        c = lax.broadcasted_iota(I32, (new, SKEYS), 1)
        for h in range(ATT_HEADS):
            t = _toeplitz_bias(w_ref[h:h + 1, :], new)
            bias_sc[h] = jnp.where(c < past + new, t[:, :SKEYS], NEG_INF)

    pad = jnp.zeros((SKEYS - past - new, HEAD_DIM), BF16)
    scale = HEAD_DIM ** -0.5
    for h in range(ATT_HEADS):
        hs = slice(h * HEAD_DIM, (h + 1) * HEAD_DIM)
        head_rows = pl.ds(h, past, stride=ATT_HEADS)
        kh = jnp.concatenate([kc_ref[0, head_rows, :].astype(BF16), kn_ref[:, hs].astype(BF16), pad], axis=0)
        vh = jnp.concatenate([vc_ref[0, head_rows, :].astype(BF16), vn_ref[:, hs].astype(BF16), pad], axis=0)
        qh = q_ref[:, hs].astype(BF16)
        s = lax.dot_general(qh, kh, (((1,), (1,)), ((), ())),
                            preferred_element_type=F32) * scale + bias_sc[h]
        m = jnp.max(s, axis=-1, keepdims=True)
        p = jnp.exp(s - m)
        l = jnp.sum(p, axis=-1, keepdims=True)
        o = jnp.dot(p.astype(BF16), vh, preferred_element_type=F32)
        o_sc[:, hs] = o / l
    _group_rmsnorm_store(o_sc[...], g_ref, o_ref)


def _attn_sample(z, row0, cache_k, cache_v, w_rows, g_att, batch, new):
    past = cache_k.shape[1] // ATT_HEADS
    assert past == BAND_WINDOW and past + new <= SKEYS and row0 % new == 0
    blk0 = row0 // new
    return pl.pallas_call(
        functools.partial(_attn_sample_kernel, past=past, new=new),
        grid=(batch,),
        in_specs=[pl.BlockSpec((ATT_HEADS, ROLL_W), lambda b: (0, 0)),
                  pl.BlockSpec((new, ATT_WIDTH), lambda b: (blk0 + b, 0)),
                  pl.BlockSpec((new, ATT_WIDTH), lambda b: (blk0 + b, 1)),
                  pl.BlockSpec((new, ATT_WIDTH), lambda b: (blk0 + b, 2)),
                  pl.BlockSpec((1, past * ATT_HEADS, HEAD_DIM), lambda b: (b, 0, 0)),
                  pl.BlockSpec((1, past * ATT_HEADS, HEAD_DIM), lambda b: (b, 0, 0)),
                  pl.BlockSpec((1, ATT_WIDTH), lambda b: (0, 0))],
        out_specs=pl.BlockSpec((new, ATT_WIDTH), lambda b: (b, 0)),
        out_shape=jax.ShapeDtypeStruct((batch * new, ATT_WIDTH), BF16),
        scratch_shapes=[pltpu.VMEM((ATT_HEADS, new, SKEYS), F32),
                        pltpu.VMEM((new, ATT_WIDTH), F32)],
        compiler_params=_cparams(("arbitrary",), 48),
        name="attn_sample",
    )(w_rows, z, z, z, cache_k, cache_v, g_att.reshape(1, ATT_WIDTH))


def _conv_tail(u, hist, w_ref, cb_ref, lg_ref, lb_ref, gg_ref, o_ref, st_ref, u_sc, y_sc, tb):
    u_sc[0:CONV_HALO, :] = hist
    u_sc[CONV_HALO:CONV_HALO + tb, :] = u
    st_ref[0] = u[tb - CONV_HALO:, :]
    off = CONV_HALO - (CONV_K - 1)
    for cc in range(CONV_WIDTH // CONV_CW):
        cs = slice(cc * CONV_CW, (cc + 1) * CONV_CW)
        acc = u_sc[off:off + tb, cs] * w_ref[0:1, cs]
        for j in range(1, CONV_K):
            acc = acc + u_sc[off + j:off + j + tb, cs] * w_ref[j:j + 1, cs]
        y_sc[:, cs] = acc + cb_ref[:, cs]
    y = y_sc[...]
    mu = jnp.mean(y, axis=-1, keepdims=True)
    yc = y - mu
    y = yc * lax.rsqrt(jnp.mean(yc * yc, axis=-1, keepdims=True) + NORM_EPS)
    y = y * lg_ref[...] + lb_ref[...]
    y = y * jax.nn.sigmoid(y)
    _group_rmsnorm_store(y, gg_ref, o_ref)


def _conv_prompt_kernel(cv_ref, cg_ref, pv_ref, pg_ref, w_ref, cb_ref, lg_ref, lb_ref, gg_ref,
                        o_ref, st_ref, u_sc, y_sc):
    u = cv_ref[...] * jax.nn.sigmoid(cg_ref[...])
    hist = pv_ref[...] * jax.nn.sigmoid(pg_ref[...])
    hist = jnp.where(pl.program_id(1) > 0, hist, 0.0)
    _conv_tail(u, hist, w_ref, cb_ref, lg_ref, lb_ref, gg_ref, o_ref, st_ref, u_sc, y_sc, CONV_TB)


def _conv_sample_kernel(cv_ref, cg_ref, past_ref, w_ref, cb_ref, lg_ref, lb_ref, gg_ref,
                        o_ref, st_ref, u_sc, y_sc, *, tb):
    u = cv_ref[...] * jax.nn.sigmoid(cg_ref[...])
    _conv_tail(u, past_ref[0], w_ref, cb_ref, lg_ref, lb_ref, gg_ref, o_ref, st_ref, u_sc, y_sc, tb)


def _conv_param_specs(nidx):
    zero = (lambda b, t: (0, 0)) if nidx == 2 else (lambda b: (0, 0))
    return [pl.BlockSpec((CONV_K, CONV_WIDTH), zero)] + [pl.BlockSpec((1, CONV_WIDTH), zero)] * 4


def _conv_prompt(z, params, batch, seq):
    ntb = seq // CONV_TB
    per = CONV_TB // CONV_HALO
    cv_blk, cg_blk = 3 * ATT_WIDTH // CONV_WIDTH, 3 * ATT_WIDTH // CONV_WIDTH + 1

    def cur(col):
        return pl.BlockSpec((CONV_TB, CONV_WIDTH), lambda b, t: (b * ntb + t, col))

    def prev(col):
        return pl.BlockSpec((CONV_HALO, CONV_WIDTH),
                            lambda b, t: (jnp.maximum((b * ntb + t) * per - 1, 0), col))

    return pl.pallas_call(
        _conv_prompt_kernel,
        grid=(batch, ntb),
        in_specs=[cur(cv_blk), cur(cg_blk), prev(cv_blk), prev(cg_blk)] + _conv_param_specs(2),
        out_specs=[pl.BlockSpec((CONV_TB, CONV_WIDTH), lambda b, t: (b * ntb + t, 0)),
                   pl.BlockSpec((1, CONV_HALO, CONV_WIDTH), lambda b, t: (b, 0, 0))],
        out_shape=[jax.ShapeDtypeStruct((batch * seq, CONV_WIDTH), BF16),
                   jax.ShapeDtypeStruct((batch, CONV_HALO, CONV_WIDTH), F32)],
        scratch_shapes=[pltpu.VMEM((CONV_HALO + CONV_TB, CONV_WIDTH), F32),
                        pltpu.VMEM((CONV_TB, CONV_WIDTH), F32)],
        compiler_params=_cparams(("arbitrary", "arbitrary"), 48),
        name="conv_prompt",
    )(z, z, z, z, *params)


def _conv_sample(z, row0, past, params, batch, new):
    assert row0 % new == 0 and new >= CONV_HALO
    blk0 = row0 // new
    cv_blk, cg_blk = 3 * ATT_WIDTH // CONV_WIDTH, 3 * ATT_WIDTH // CONV_WIDTH + 1
    return pl.pallas_call(
        functools.partial(_conv_sample_kernel, tb=new),
        grid=(batch,),
        in_specs=[pl.BlockSpec((new, CONV_WIDTH), lambda b: (blk0 + b, cv_blk)),
                  pl.BlockSpec((new, CONV_WIDTH), lambda b: (blk0 + b, cg_blk)),
                  pl.BlockSpec((1, CONV_HALO, CONV_WIDTH), lambda b: (b, 0, 0))] + _conv_param_specs(1),
        out_specs=[pl.BlockSpec((new, CONV_WIDTH), lambda b: (b, 0)),
                   pl.BlockSpec((1, CONV_HALO, CONV_WIDTH), lambda b: (b, 0, 0))],
        out_shape=[jax.ShapeDtypeStruct((batch * new, CONV_WIDTH), BF16),
                   jax.ShapeDtypeStruct((batch, CONV_HALO, CONV_WIDTH), F32)],
        scratch_shapes=[pltpu.VMEM((CONV_HALO + new, CONV_WIDTH), F32),
                        pltpu.VMEM((new, CONV_WIDTH), F32)],
        compiler_params=_cparams(("arbitrary",), 32),
        name="conv_sample",
    )(z, z, past, *params)


def _memattn_kernel(q_ref, mk_ref, mv_ref, o_ref):
    mk = mk_ref[0].astype(BF16)
    mv = mv_ref[0].astype(BF16)
    scale = HEAD_DIM ** -0.5
    for h in range(MEM_HEADS):
        hs = slice(h * HEAD_DIM, (h + 1) * HEAD_DIM)
        qh = q_ref[:, hs].astype(BF16)
        s = lax.dot_general(qh, mk[:, hs], (((1,), (1,)), ((), ())),
                            preferred_element_type=F32) * scale
        m = jnp.max(s, axis=-1, keepdims=True)
        p = jnp.exp(s - m)
        l = jnp.sum(p, axis=-1, keepdims=True)
        o = jnp.dot(p.astype(BF16), mv[:, hs], preferred_element_type=F32)
        o_ref[:, hs] = (o / l).astype(o_ref.dtype)


def _memattn(q, row0, mk, mv, k_col, v_col, batch, rows_per_batch, tb, name):
    nb = rows_per_batch // tb
    blk0 = row0 // tb
    return pl.pallas_call(
        _memattn_kernel,
        grid=(batch, nb),
        in_specs=[pl.BlockSpec((tb, MEM_WIDTH), lambda b, t: (blk0 + b * nb + t, 0)),
                  pl.BlockSpec((1, MEM_TOKENS, MEM_WIDTH), lambda b, t: (b, 0, k_col)),
                  pl.BlockSpec((1, MEM_TOKENS, MEM_WIDTH), lambda b, t: (b, 0, v_col))],
        out_specs=pl.BlockSpec((tb, MEM_WIDTH), lambda b, t: (b * nb + t, 0)),
        out_shape=jax.ShapeDtypeStruct((batch * rows_per_batch, MEM_WIDTH), BF16),
        compiler_params=_cparams(("arbitrary", "arbitrary"), 32),
        name=name,
    )(q, mk, mv)


def _router_kernel(h_ref, w_ref, b_ref, idx_ref, gate_ref):
    logits = jnp.dot(h_ref[...], w_ref[...], preferred_element_type=F32,
                     precision=lax.Precision.HIGHEST) + b_ref[...]
    lane = lax.broadcasted_iota(I32, logits.shape, 1)
    logits = jnp.where(lane < N_EXPERTS, logits, -jnp.inf)
    vals, idxs = [], []
    for _ in range(TOP_K):
        m = jnp.max(logits, axis=-1, keepdims=True)
        idx = jnp.min(jnp.where(logits == m, lane, LANES), axis=-1, keepdims=True)
        vals.append(m)
        idxs.append(idx)
        logits = jnp.where(lane == idx, -jnp.inf, logits)
    es = [jnp.exp(v - vals[0]) for v in vals]
    tot = es[0]
    for e in es[1:]:
        tot = tot + e
    idx_out = jnp.zeros(logits.shape, I32)
    gate_out = jnp.zeros(logits.shape, F32)
    for k in range(TOP_K):
        idx_out = jnp.where(lane == k, idxs[k], idx_out)
        gate_out = jnp.where(lane == k, es[k] / tot, gate_out)
    idx_ref[...] = idx_out
    gate_ref[...] = gate_out


def _router(h, w_router, b_router):
    rows, d = h.shape
    tm = NORM_ROWS
    wp = jnp.zeros((d, LANES), F32).at[:, :N_EXPERTS].set(w_router)
    bp = jnp.zeros((1, LANES), F32).at[0, :N_EXPERTS].set(b_router)
    return pl.pallas_call(
        _router_kernel,
        grid=(rows // tm,),
        in_specs=[pl.BlockSpec((tm, d), lambda i: (i, 0)),
                  pl.BlockSpec((d, LANES), lambda i: (0, 0)),
                  pl.BlockSpec((1, LANES), lambda i: (0, 0))],
        out_specs=[pl.BlockSpec((tm, LANES), lambda i: (i, 0)),
                   pl.BlockSpec((tm, LANES), lambda i: (i, 0))],
        out_shape=[jax.ShapeDtypeStruct((rows, LANES), I32),
                   jax.ShapeDtypeStruct((rows, LANES), F32)],
        compiler_params=_cparams(("arbitrary",), 40),
        name="router",
    )(h, wp, bp)


def _moe_tiles(n_assign):
    s_max = n_assign // MOE_NMAX + N_EXPERTS
    return s_max, s_max * MOE_NMAX


def _route(top_idx):
    n = top_idx.size
    s_max, rows_tot = _moe_tiles(n)
    e_flat = top_idx.reshape(-1)
    iota = jnp.arange(n, dtype=I32)
    _, order = lax.sort((e_flat, iota), num_keys=1, is_stable=True)
    _, rank_all = lax.sort((order, iota), num_keys=1)
    counts = jnp.sum((e_flat[:, None] == jnp.arange(N_EXPERTS, dtype=I32)[None, :]).astype(I32), axis=0)
    start = jnp.cumsum(counts) - counts
    n_st = (counts + MOE_NMAX - 1) // MOE_NMAX
    share = (counts + jnp.maximum(n_st, 1) - 1) // jnp.maximum(n_st, 1)
    q = jnp.maximum((share + MOE_R - 1) // MOE_R * MOE_R, MOE_R)
    st_end = jnp.cumsum(n_st)
    st_start = st_end - n_st
    n_live = st_end[-1]
    s_ids = jnp.arange(s_max, dtype=I32)
    st_e = jnp.minimum(jnp.searchsorted(st_end, s_ids, side="right"), N_EXPERTS - 1).astype(I32)
    t_in_e = s_ids - st_start[st_e]
    st_rows = jnp.clip(counts[st_e] - t_in_e * q[st_e], 0, q[st_e])
    st_rows = jnp.where(s_ids < n_live, st_rows, 0).astype(I32)
    st_src = start[st_e] + t_in_e * q[st_e]
    p = jnp.arange(rows_tot, dtype=I32)
    tile, r = p // MOE_NMAX, p % MOE_NMAX
    src = jnp.clip(st_src[tile] + r, 0, n - 1)
    row_tok = jnp.where(r < st_rows[tile], order[src] // TOP_K, 0).astype(I32)
    rank = rank_all - start[e_flat]
    pos = ((st_start[e_flat] + rank // q[e_flat]) * MOE_NMAX + rank % q[e_flat]).astype(I32)
    return dict(st_e=st_e, st_rows=st_rows, n_live=n_live.reshape(1).astype(I32), row_tok=row_tok, pos=pos)


def _gather_kernel(st_n, nblk_ref, tok_ref, tokn_ref, h_hbm, o_ref, buf, sem):
    i = pl.program_id(0)
    per = MOE_NMAX // GATHER_ROWS
    nblk = nblk_ref[0]
    slot = i % 2

    def live(blk):
        return st_n[blk // per] > (blk % per) * GATHER_ROWS

    def row_copy(tok, k, s):
        return pltpu.make_async_copy(h_hbm.at[pl.ds(tok, 1), :], buf.at[s, pl.ds(k, 1), :], sem.at[s])

    def issue(tok_smem, s):
        def body(k, c):
            row_copy(tok_smem[k], k, s).start()
            return c
        lax.fori_loop(0, GATHER_ROWS, body, 0, unroll=DMA_UNROLL)

    @pl.when((i == 0) & live(0))
    def _():
        issue(tok_ref, 0)

    nxt = jnp.minimum(i + 1, nblk - 1)

    @pl.when((i + 1 < nblk) & live(nxt))
    def _():
        issue(tokn_ref, 1 - slot)

    @pl.when(live(i))
    def _():
        def wait(k, c):
            row_copy(0, k, slot).wait()
            return c
        lax.fori_loop(0, GATHER_ROWS, wait, 0, unroll=DMA_UNROLL)
        o_ref[...] = buf[slot].astype(o_ref.dtype)


def _gather(h, route, rows_tot):
    d = h.shape[1]
    per = MOE_NMAX // GATHER_ROWS
    nblk = route["n_live"] * per
    return pl.pallas_call(
        _gather_kernel,
        grid_spec=pltpu.PrefetchScalarGridSpec(
            num_scalar_prefetch=2,
            grid=(nblk[0],),
            in_specs=[pl.BlockSpec((GATHER_ROWS,), lambda i, sn, nb: (i,), memory_space=pltpu.SMEM),
                      pl.BlockSpec((GATHER_ROWS,), lambda i, sn, nb: (jnp.minimum(i + 1, nb[0] - 1),),
                                   memory_space=pltpu.SMEM),
                      pl.BlockSpec(memory_space=pl.ANY)],
            out_specs=pl.BlockSpec((GATHER_ROWS, d), lambda i, sn, nb: (i, 0)),
            scratch_shapes=[pltpu.VMEM((2, GATHER_ROWS, d), F32), pltpu.SemaphoreType.DMA((2,))]),
        out_shape=jax.ShapeDtypeStruct((rows_tot, d), BF16),
        compiler_params=_cparams(("arbitrary",), 32),
        name="moe_gather",
    )(route["st_rows"], nblk, route["row_tok"], route["row_tok"], h)


def _row_passes(cnt):
    return (cnt + MOE_R - 1) // MOE_R


def _moe_up_kernel(st_e, st_n, xs_ref, wg_ref, wu_ref, bg_ref, bu_ref, o_ref, wgb, wub):
    npass = _row_passes(st_n[pl.program_id(0)])
    wgb[...] = wg_ref[0].astype(BF16)
    wub[...] = wu_ref[0].astype(BF16)

    def body(r, c):
        rows = pl.ds(pl.multiple_of(r * MOE_R, MOE_R), MOE_R)
        x = xs_ref[rows, :]
        g = jnp.dot(x, wgb[...], preferred_element_type=F32) + bg_ref[0]
        u = jnp.dot(x, wub[...], preferred_element_type=F32) + bu_ref[0]
        g = jnp.minimum(g, SWIGLU_LIMIT)
        u = jnp.clip(u, -SWIGLU_LIMIT, SWIGLU_LIMIT)
        o_ref[rows, :] = ((u + 1.0) * (g * jax.nn.sigmoid(SWIGLU_ALPHA * g))).astype(o_ref.dtype)
        return c

    lax.fori_loop(0, npass, body, 0)


def _moe_down_kernel(st_e, st_n, a_ref, wd_ref, bd_ref, o_ref, wdb):
    npass = _row_passes(st_n[pl.program_id(0)])
    wdb[...] = wd_ref[0].astype(BF16)

    def body(r, c):
        rows = pl.ds(pl.multiple_of(r * MOE_R, MOE_R), MOE_R)
        o_ref[rows, :] = jnp.dot(a_ref[rows, :], wdb[...], preferred_element_type=F32) + bd_ref[0]
        return c

    lax.fori_loop(0, npass, body, 0)


def _w_cols(off):
    return lambda s, j, st_e, st_n: (st_e[s], 0, off + j)


def _moe_up(xs, route, w_gate_up, b_gate_up):
    rows_tot, d = xs.shape
    nj = D_EXPERT // MOE_TN_UP
    bias = b_gate_up.reshape(N_EXPERTS, 1, 2 * D_EXPERT)
    return pl.pallas_call(
        _moe_up_kernel,
        grid_spec=pltpu.PrefetchScalarGridSpec(
            num_scalar_prefetch=2,
            grid=(route["n_live"][0], nj),
            in_specs=[pl.BlockSpec((MOE_NMAX, d), lambda s, j, st_e, st_n: (s, 0)),
                      pl.BlockSpec((1, d, MOE_TN_UP), _w_cols(0)),
                      pl.BlockSpec((1, d, MOE_TN_UP), _w_cols(nj)),
                      pl.BlockSpec((1, 1, MOE_TN_UP), _w_cols(0)),
                      pl.BlockSpec((1, 1, MOE_TN_UP), _w_cols(nj))],
            out_specs=pl.BlockSpec((MOE_NMAX, MOE_TN_UP), lambda s, j, st_e, st_n: (s, j)),
            scratch_shapes=[pltpu.VMEM((d, MOE_TN_UP), BF16), pltpu.VMEM((d, MOE_TN_UP), BF16)]),
        out_shape=jax.ShapeDtypeStruct((rows_tot, D_EXPERT), BF16),
        compiler_params=_cparams(("arbitrary", "arbitrary"), 52),
        name="moe_up",
    )(route["st_e"], route["st_rows"], xs, w_gate_up, w_gate_up, bias, bias)


def _moe_down(act, route, w_down, b_down):
    rows_tot, f = act.shape
    d = w_down.shape[2]
    nj = d // MOE_TN_DOWN
    return pl.pallas_call(
        _moe_down_kernel,
        grid_spec=pltpu.PrefetchScalarGridSpec(
            num_scalar_prefetch=2,
            grid=(route["n_live"][0], nj),
            in_specs=[pl.BlockSpec((MOE_NMAX, f), lambda s, j, st_e, st_n: (s, 0)),
                      pl.BlockSpec((1, f, MOE_TN_DOWN), _w_cols(0)),
                      pl.BlockSpec((1, 1, MOE_TN_DOWN), _w_cols(0))],
            out_specs=pl.BlockSpec((MOE_NMAX, MOE_TN_DOWN), lambda s, j, st_e, st_n: (s, j)),
            scratch_shapes=[pltpu.VMEM((f, MOE_TN_DOWN), BF16)]),
        out_shape=jax.ShapeDtypeStruct((rows_tot, d), F32),
        compiler_params=_cparams(("arbitrary", "arbitrary"), 52),
        name="moe_down",
    )(route["st_e"], route["st_rows"], act, w_down, b_down.reshape(N_EXPERTS, 1, d))


def _combine_kernel(pos_ref, posn_ref, ys_hbm, x_ref, gate_ref, g_ref, op_ref, os_ref, buf, sem,
                    *, n_blk, n_prompt_blk):
    i = pl.program_id(0)
    slot = i % 2
    n = COMBINE_TOK * TOP_K

    def row_copy(p, a, s):
        return pltpu.make_async_copy(ys_hbm.at[pl.ds(p, 1), :],
                                     buf.at[s, a % TOP_K, pl.ds(a // TOP_K, 1), :], sem.at[s])

    def issue(pos_smem, s):
        def body(a, c):
            row_copy(pos_smem[a], a, s).start()
            return c
        lax.fori_loop(0, n, body, 0, unroll=DMA_UNROLL)

    @pl.when(i == 0)
    def _():
        issue(pos_ref, 0)

    @pl.when(i + 1 < n_blk)
    def _():
        issue(posn_ref, 1 - slot)

    def wait(a, c):
        row_copy(0, a, slot).wait()
        return c

    lax.fori_loop(0, n, wait, 0, unroll=DMA_UNROLL)
    x = x_ref[...]
    for k in range(TOP_K):
        x = x + buf[slot, k] * gate_ref[:, k:k + 1]
    ms = jnp.mean(x * x, axis=-1, keepdims=True)
    y = x * lax.rsqrt(ms + NORM_EPS) * g_ref[...]

    @pl.when(i < n_prompt_blk)
    def _():
        op_ref[...] = y

    @pl.when(i >= n_prompt_blk)
    def _():
        os_ref[...] = y


def _combine(ys, pos, x, gates, g_final, rows_prompt):
    rows, d = x.shape
    n_blk = rows // COMBINE_TOK
    npb = rows_prompt // COMBINE_TOK
    nidx = COMBINE_TOK * TOP_K
    return pl.pallas_call(
        functools.partial(_combine_kernel, n_blk=n_blk, n_prompt_blk=npb),
        grid=(n_blk,),
        in_specs=[pl.BlockSpec((nidx,), lambda i: (i,), memory_space=pltpu.SMEM),
                  pl.BlockSpec((nidx,), lambda i: (jnp.minimum(i + 1, n_blk - 1),), memory_space=pltpu.SMEM),
                  pl.BlockSpec(memory_space=pl.ANY),
                  pl.BlockSpec((COMBINE_TOK, d), lambda i: (i, 0)),
                  pl.BlockSpec((COMBINE_TOK, LANES), lambda i: (i, 0)),
                  pl.BlockSpec((1, d), lambda i: (0, 0))],
        out_specs=[pl.BlockSpec((COMBINE_TOK, d), lambda i: (jnp.minimum(i, npb - 1), 0)),
                   pl.BlockSpec((COMBINE_TOK, d), lambda i: (jnp.maximum(i - npb, 0), 0))],
        out_shape=[jax.ShapeDtypeStruct((rows_prompt, d), F32),
                   jax.ShapeDtypeStruct((rows - rows_prompt, d), F32)],
        scratch_shapes=[pltpu.VMEM((2, TOP_K, COMBINE_TOK, d), F32), pltpu.SemaphoreType.DMA((2,))],
        compiler_params=_cparams(("arbitrary",), 32),
        name="moe_combine",
    )(pos, pos, ys, x, gates, g_final.reshape(1, d))


def _tail_rows(z, batch, seq, nrows, c0, c1):
    return jnp.stack([lax.slice(z, ((b + 1) * seq - nrows, c0), ((b + 1) * seq, c1)) for b in range(batch)])


def kernel(x_prompt, x_sample, cache_band_k, cache_band_v, state_conv, cache_mem_k, cache_mem_v, mem_prompt, norm_mix, w_in, rel_table, conv_w, conv_b, conv_ln_g, conv_ln_b, grp_g_att, grp_g_conv, w_out, norm_mem_q, norm_mem_kv, w_mem_q, w_mem_k, w_mem_v, w_mem_o, norm_ffn, w_router, b_router, w_gate_up, b_gate_up, w_down, b_down, norm_final):
    depth = norm_mix.shape[0]
    assert depth == 1
    bp, sp, d = x_prompt.shape
    bs, ss, _ = x_sample.shape
    tp, ts = bp * sp, bs * ss
    t_all = tp + ts
    assert d == D_MODEL and sp % QB == 0 and sp % CONV_TB == 0 and tp % MM_ROWS == 0
    assert sp >= BAND_WINDOW and ss == CHUNK and t_all % MM_ROWS == 0
    assert tp % COMBINE_TOK == 0 and ts % COMBINE_TOK == 0
    l = 0

    x = jnp.concatenate([x_prompt.reshape(tp, d), x_sample.reshape(ts, d)], axis=0)

    h = _rmsnorm(x, norm_mix[l], BF16)
    z = _matmul([h], w_in[l].astype(BF16), None, F32, 1024, "in_proj")
    w_rows = _rel_rows(rel_table[l])
    att_p = _attn_prompt(z, w_rows, grp_g_att[l], bp, sp)
    ck = cache_band_k[l].reshape(bs, -1, HEAD_DIM)
    cv = cache_band_v[l].reshape(bs, -1, HEAD_DIM)
    att_s = _attn_sample(z, tp, ck, cv, w_rows, grp_g_att[l], bs, ss)
    conv_params = (conv_w[l], conv_b[l].reshape(1, -1), conv_ln_g[l].reshape(1, -1),
                   conv_ln_b[l].reshape(1, -1), grp_g_conv[l].reshape(1, -1))
    cnv_p, cs_p = _conv_prompt(z, conv_params, bp, sp)
    past = jnp.pad(state_conv[l], ((0, 0), (CONV_HALO - (CONV_K - 1), 0), (0, 0)))
    cnv_s, cs_s = _conv_sample(z, tp, past, conv_params, bs, ss)
    att = jnp.concatenate([att_p, att_s], axis=0)
    cnv = jnp.concatenate([cnv_p, cnv_s], axis=0)
    x1 = _matmul([att, cnv], w_out[l].astype(BF16), x, F32, 512, "out_proj")

    hd = (ATT_HEADS, HEAD_DIM)
    bk_p = _tail_rows(z, bp, sp, BAND_WINDOW, ATT_WIDTH, 2 * ATT_WIDTH).reshape(1, bp, BAND_WINDOW, *hd)
    bv_p = _tail_rows(z, bp, sp, BAND_WINDOW, 2 * ATT_WIDTH, 3 * ATT_WIDTH).reshape(1, bp, BAND_WINDOW, *hd)
    bk_s = lax.slice(z, (tp, ATT_WIDTH), (t_all, 2 * ATT_WIDTH)).reshape(1, bs, ss, *hd)
    bv_s = lax.slice(z, (tp, 2 * ATT_WIDTH), (t_all, 3 * ATT_WIDTH)).reshape(1, bs, ss, *hd)
    keep = CONV_HALO - (CONV_K - 1)
    cs_p = cs_p[:, keep:][None]
    cs_s = cs_s[:, keep:][None]

    nm = mem_prompt.shape[1]
    mem_n = _rmsnorm(mem_prompt.reshape(bp * nm, d), norm_mem_kv[l], BF16)
    w_mkv = jnp.concatenate([w_mem_k[l], w_mem_v[l]], axis=1).astype(BF16)
    mkv = _matmul([mem_n], w_mkv, None, F32, 2 * MEM_WIDTH, "mem_kv").reshape(bp, nm, 2 * MEM_WIDTH)
    mk_p = mkv[:, :, :MEM_WIDTH].reshape(1, bp, nm, MEM_HEADS, HEAD_DIM)
    mv_p = mkv[:, :, MEM_WIDTH:].reshape(1, bp, nm, MEM_HEADS, HEAD_DIM)
    hq = _rmsnorm(x1, norm_mem_q[l], BF16)
    q = _matmul([hq], w_mem_q[l].astype(BF16), None, F32, MEM_WIDTH, "mem_q")
    mo_p = _memattn(q, 0, mkv, mkv, 0, 1, bp, sp, 512, "memattn_prompt")
    cmk = cache_mem_k[l].reshape(bs, nm, MEM_WIDTH)
    cmv = cache_mem_v[l].reshape(bs, nm, MEM_WIDTH)
    mo_s = _memattn(q, tp, cmk, cmv, 0, 0, bs, ss, ss, "memattn_sample")
    mo = jnp.concatenate([mo_p, mo_s], axis=0)
    x2 = _matmul([mo], w_mem_o[l].astype(BF16), x1, F32, 512, "mem_o")

    hf = _rmsnorm(x2, norm_ffn[l], F32)
    top_idx, gates = _router(hf, w_router[l], b_router[l])
    route = _route(top_idx[:, :TOP_K])
    _, rows_tot = _moe_tiles(t_all * TOP_K)
    xs = _gather(hf, route, rows_tot)
    act = _moe_up(xs, route, w_gate_up[l], b_gate_up[l])
    ys = _moe_down(act, route, w_down[l], b_down[l])
    y_p, y_s = _combine(ys, route["pos"], x2, gates, norm_final, tp)

    return (y_p.reshape(bp, sp, d), y_s.reshape(bs, ss, d), bk_p, bv_p, cs_p, mk_p, mv_p, bk_s, bv_s, cs_s)
```

```python
import functools

import jax
import jax.numpy as jnp
from jax import lax
from jax.experimental import pallas as pl
from jax.experimental.pallas import tpu as pltpu

F32 = jnp.float32
BF16 = jnp.bfloat16
I32 = jnp.int32

D_MODEL = 4096
CHUNK = 64
LEFT_CHUNKS = 8
BAND_WINDOW = LEFT_CHUNKS * CHUNK
HEAD_DIM = 128
ATT_WIDTH = 2048
CONV_WIDTH = 2048
ATT_HEADS = 16
IN_WIDTH = 3 * ATT_WIDTH + 2 * CONV_WIDTH
MAX_REL = 256
CONV_K = 31
MEM_TOKENS = 256
MEM_HEADS = 4
MEM_WIDTH = 512
N_EXPERTS = 32
TOP_K = 4
D_EXPERT = 4096
SWIGLU_LIMIT = 7.0
SWIGLU_ALPHA = 1.702
NORM_EPS = 1e-5
NEG_INF = -1e30

LANES = 128

NORM_ROWS = 512
MM_ROWS = 1024
ATTN_HG = 8
QB = 256
KWIN = QB + BAND_WINDOW
ROLL_W = 1024
SKEYS = 640
CONV_TB = 256
CONV_HALO = 32
CONV_CW = 512
MOE_R = 256
MOE_NMAX = 1280
MOE_TN_UP = 256
MOE_TN_DOWN = 512
GATHER_ROWS = MOE_R
COMBINE_TOK = 64
DMA_UNROLL = 8


def _cparams(sem, vmem_mib):
    return pltpu.CompilerParams(dimension_semantics=sem, vmem_limit_bytes=vmem_mib * 2**20)


def _pair_specs(pair, tm, width, col_map=None):
    first, second = pair
    assert first.shape[0] % tm == 0 and second.shape[0] % tm == 0
    nf = first.shape[0] // tm
    col = col_map if col_map is not None else (lambda *ij: 0)
    specs = [pl.BlockSpec((tm, width), lambda *ij: (jnp.minimum(ij[0], nf - 1), col(*ij))),
             pl.BlockSpec((tm, width), lambda *ij: (jnp.maximum(ij[0] - nf, 0), col(*ij)))]
    return specs, nf


def _pair_value(refs, nf):
    return jnp.where(pl.program_id(0) < nf, refs[0][...], refs[1][...])


def _rmsnorm_kernel(*refs, nf):
    g_ref, o_ref = refs[-2], refs[-1]
    x = refs[0][...] if nf is None else _pair_value(refs[:2], nf)
    ms = jnp.mean(x * x, axis=-1, keepdims=True)
    o_ref[...] = (x * lax.rsqrt(ms + NORM_EPS) * g_ref[...]).astype(o_ref.dtype)


def _rmsnorm(x, g, out_dtype):
    if isinstance(x, tuple):
        rows, d = x[0].shape[0] + x[1].shape[0], x[0].shape[1]
        tm = NORM_ROWS // 2
        x_specs, nf = _pair_specs(x, tm, d)
        xs = list(x)
    else:
        rows, d = x.shape
        tm = min(NORM_ROWS, rows)
        x_specs, nf, xs = [pl.BlockSpec((tm, d), lambda i: (i, 0))], None, [x]
    return pl.pallas_call(
        functools.partial(_rmsnorm_kernel, nf=nf),
        grid=(rows // tm,),
        in_specs=x_specs + [pl.BlockSpec((1, d), lambda i: (0, 0))],
        out_specs=pl.BlockSpec((tm, d), lambda i: (i, 0)),
        out_shape=jax.ShapeDtypeStruct((rows, d), out_dtype),
        compiler_params=_cparams(("arbitrary",), 48),
        name="rmsnorm",
    )(*xs, g.reshape(1, d))


def _mm_kernel(*refs, layout, nf):
    a_counts, res_count = layout[:-1], layout[-1]
    pos = 0
    a_vals = []
    for c in a_counts:
        a_vals.append(refs[pos][...] if c == 1 else _pair_value(refs[pos:pos + 2], nf))
        pos += c
    b_refs = refs[pos:pos + len(a_counts)]
    pos += len(a_counts)
    acc = jnp.dot(a_vals[0], b_refs[0][...], preferred_element_type=F32)
    for k in range(1, len(a_vals)):
        acc = acc + jnp.dot(a_vals[k], b_refs[k][...], preferred_element_type=F32)
    if res_count:
        acc = acc + (refs[pos][...] if res_count == 1 else _pair_value(refs[pos:pos + 2], nf))
    refs[-1][...] = acc.astype(refs[-1].dtype)


def _rows_of(x):
    return x[0].shape[0] + x[1].shape[0] if isinstance(x, tuple) else x.shape[0]


def _matmul(a_parts, w, res, out_dtype, tn, name, tm=MM_ROWS):
    m = _rows_of(a_parts[0])
    kk = (a_parts[0][0] if isinstance(a_parts[0], tuple) else a_parts[0]).shape[1]
    n = w.shape[1]
    tm = min(tm, m)
    in_specs, args, layout, nf = [], [], [], None
    for a in a_parts:
        if isinstance(a, tuple):
            specs, nf = _pair_specs(a, tm, kk)
            in_specs += specs
            args += list(a)
            layout.append(2)
        else:
            in_specs.append(pl.BlockSpec((tm, kk), lambda i, j: (i, 0)))
            args.append(a)
            layout.append(1)
    in_specs += [pl.BlockSpec((kk, tn), functools.partial(lambda i, j, p: (p, j), p=p))
                 for p in range(len(a_parts))]
    args += [w] * len(a_parts)
    if isinstance(res, tuple):
        specs, nf = _pair_specs(res, tm, tn, col_map=lambda i, j: j)
        in_specs += specs
        args += list(res)
        layout.append(2)
    elif res is not None:
        in_specs.append(pl.BlockSpec((tm, tn), lambda i, j: (i, j)))
        args.append(res)
        layout.append(1)
    else:
        layout.append(0)
    return pl.pallas_call(
        functools.partial(_mm_kernel, layout=tuple(layout), nf=nf),
        grid=(m // tm, n // tn),
        in_specs=in_specs,
        out_specs=pl.BlockSpec((tm, tn), lambda i, j: (i, j)),
        out_shape=jax.ShapeDtypeStruct((m, n), out_dtype),
        compiler_params=_cparams(("arbitrary", "arbitrary"), 56),
        name=name,
    )(*args)


def _rel_rows(table):
    m = jnp.arange(ROLL_W)
    m = jnp.where(m < KWIN, m, m - ROLL_W)
    idx = jnp.clip(BAND_WINDOW - m, -MAX_REL, MAX_REL) + MAX_REL
    return table[:, idx].astype(F32)


def _toeplitz_bias(w_row, rows):
    wb = jnp.broadcast_to(w_row, (rows, ROLL_W))
    return pltpu.roll(wb, 0, 1, stride=1, stride_axis=0)


def _group_rmsnorm_store(o, g_ref, o_ref):
    ms = jnp.mean(o * o, axis=-1, keepdims=True)
    o_ref[...] = (o * lax.rsqrt(ms + NORM_EPS) * g_ref[...]).astype(o_ref.dtype)


def _attn_prompt_kernel(w_ref, q_ref, k0_ref, k1_ref, k2_ref, v0_ref, v1_ref, v2_ref, g_ref,
                        o_ref, bias_sc, o_sc):
    i = pl.program_id(1)
    hg = pl.program_id(2)
    first = (pl.program_id(0) == 0) & (i == 0) & (hg == 0)

    @pl.when(first)
    def _():
        r = lax.broadcasted_iota(I32, (QB, KWIN), 0) // CHUNK
        c = lax.broadcasted_iota(I32, (QB, KWIN), 1) // CHUNK
        band = (c >= r) & (c <= r + LEFT_CHUNKS)
        for h in range(ATT_HEADS):
            t = _toeplitz_bias(w_ref[h:h + 1, :], QB)
            bias_sc[h] = jnp.where(band, t[:, :KWIN], NEG_INF)

    kk = jnp.concatenate([r[...].astype(BF16) for r in (k0_ref, k1_ref, k2_ref)], axis=0)
    vv = jnp.concatenate([r[...].astype(BF16) for r in (v0_ref, v1_ref, v2_ref)], axis=0)
    col = lax.broadcasted_iota(I32, (QB, KWIN), 1)
    in_seq = col >= (BAND_WINDOW // QB - i) * QB
    scale = HEAD_DIM ** -0.5
    for hh in range(ATTN_HG):
        hs = slice(hh * HEAD_DIM, (hh + 1) * HEAD_DIM)
        qh = q_ref[:, hs].astype(BF16)
        s = lax.dot_general(qh, kk[:, hs], (((1,), (1,)), ((), ())),
                            preferred_element_type=F32) * scale + bias_sc[hg * ATTN_HG + hh]
        s = jnp.where(in_seq, s, NEG_INF)
        m = jnp.max(s, axis=-1, keepdims=True)
        p = jnp.exp(s - m)
        l = jnp.sum(p, axis=-1, keepdims=True)
        o = jnp.dot(p.astype(BF16), vv[:, hs], preferred_element_type=F32)
        o_sc[hg, :, hs] = o / l

    @pl.when(hg == ATT_HEADS // ATTN_HG - 1)
    def _():
        o = jnp.concatenate([o_sc[g] for g in range(ATT_HEADS // ATTN_HG)], axis=1)
        _group_rmsnorm_store(o, g_ref, o_ref)


def _attn_prompt(z, w_rows, g_att, batch, seq):
    nqb = seq // QB
    back = BAND_WINDOW // QB
    gw = ATTN_HG * HEAD_DIM
    ngrp = ATT_HEADS // ATTN_HG

    def kv_spec(col0, d):
        return pl.BlockSpec((QB, gw),
                            lambda b, i, g: (b * nqb + jnp.maximum(i - d, 0), col0 + g))

    kc, vc = ATT_WIDTH // gw, 2 * ATT_WIDTH // gw
    return pl.pallas_call(
        _attn_prompt_kernel,
        grid=(batch, nqb, ngrp),
        in_specs=[pl.BlockSpec((ATT_HEADS, ROLL_W), lambda b, i, g: (0, 0)),
                  pl.BlockSpec((QB, gw), lambda b, i, g: (b * nqb + i, g)),
                  kv_spec(kc, back), kv_spec(kc, back - 1), kv_spec(kc, 0),
                  kv_spec(vc, back), kv_spec(vc, back - 1), kv_spec(vc, 0),
                  pl.BlockSpec((1, ATT_WIDTH), lambda b, i, g: (0, 0))],
        out_specs=pl.BlockSpec((QB, ATT_WIDTH), lambda b, i, g: (b * nqb + i, 0)),
        out_shape=jax.ShapeDtypeStruct((batch * seq, ATT_WIDTH), BF16),
        scratch_shapes=[pltpu.VMEM((ATT_HEADS, QB, KWIN), F32),
                        pltpu.VMEM((ngrp, QB, gw), F32)],
        compiler_params=_cparams(("arbitrary", "arbitrary", "arbitrary"), 48),
        name="attn_prompt",
    )(w_rows, z, z, z, z, z, z, z, g_att.reshape(1, ATT_WIDTH))


def _attn_sample_kernel(w_ref, q_ref, kn_ref, vn_ref, kc_ref, vc_ref, g_ref, o_ref, bias_sc, o_sc,
                        *, past, new):
    @pl.when(pl.program_id(0) == 0)
    def _():
        c = lax.broadcasted_iota(I32, (new, SKEYS), 1)
        for h in range(ATT_HEADS):
            t = _toeplitz_bias(w_ref[h:h + 1, :], new)
            bias_sc[h] = jnp.where(c < past + new, t[:, :SKEYS], NEG_INF)

    pad = jnp.zeros((SKEYS - past - new, HEAD_DIM), BF16)
    scale = HEAD_DIM ** -0.5
    for h in range(ATT_HEADS):
        hs = slice(h * HEAD_DIM, (h + 1) * HEAD_DIM)
        head_rows = pl.ds(h, past, stride=ATT_HEADS)
        kh = jnp.concatenate([kc_ref[0, head_rows, :].astype(BF16), kn_ref[:, hs].astype(BF16), pad], axis=0)
        vh = jnp.concatenate([vc_ref[0, head_rows, :].astype(BF16), vn_ref[:, hs].astype(BF16), pad], axis=0)
        qh = q_ref[:, hs].astype(BF16)
        s = lax.dot_general(qh, kh, (((1,), (1,)), ((), ())),
                            preferred_element_type=F32) * scale + bias_sc[h]
        m = jnp.max(s, axis=-1, keepdims=True)
        p = jnp.exp(s - m)
        l = jnp.sum(p, axis=-1, keepdims=True)
        o = jnp.dot(p.astype(BF16), vh, preferred_element_type=F32)
        o_sc[:, hs] = o / l
    _group_rmsnorm_store(o_sc[...], g_ref, o_ref)


def _attn_sample(z, row0, cache_k, cache_v, w_rows, g_att, batch, new):
    past = cache_k.shape[1] // ATT_HEADS
    assert past == BAND_WINDOW and past + new <= SKEYS and row0 % new == 0
    blk0 = row0 // new
    return pl.pallas_call(
        functools.partial(_attn_sample_kernel, past=past, new=new),
        grid=(batch,),
        in_specs=[pl.BlockSpec((ATT_HEADS, ROLL_W), lambda b: (0, 0)),
                  pl.BlockSpec((new, ATT_WIDTH), lambda b: (blk0 + b, 0)),
                  pl.BlockSpec((new, ATT_WIDTH), lambda b: (blk0 + b, 1)),
                  pl.BlockSpec((new, ATT_WIDTH), lambda b: (blk0 + b, 2)),
                  pl.BlockSpec((1, past * ATT_HEADS, HEAD_DIM), lambda b: (b, 0, 0)),
                  pl.BlockSpec((1, past * ATT_HEADS, HEAD_DIM), lambda b: (b, 0, 0)),
                  pl.BlockSpec((1, ATT_WIDTH), lambda b: (0, 0))],
        out_specs=pl.BlockSpec((new, ATT_WIDTH), lambda b: (b, 0)),
        out_shape=jax.ShapeDtypeStruct((batch * new, ATT_WIDTH), BF16),
        scratch_shapes=[pltpu.VMEM((ATT_HEADS, new, SKEYS), F32),
                        pltpu.VMEM((new, ATT_WIDTH), F32)],
        compiler_params=_cparams(("arbitrary",), 48),
        name="attn_sample",
    )(w_rows, z, z, z, cache_k, cache_v, g_att.reshape(1, ATT_WIDTH))


def _conv_tail(u, hist, w_ref, cb_ref, lg_ref, lb_ref, gg_ref, o_ref, st_ref, u_sc, y_sc, tb):
    u_sc[0:CONV_HALO, :] = hist
    u_sc[CONV_HALO:CONV_HALO + tb, :] = u
    st_ref[0] = u[tb - CONV_HALO:, :]
    off = CONV_HALO - (CONV_K - 1)
    for cc in range(CONV_WIDTH // CONV_CW):
        cs = slice(cc * CONV_CW, (cc + 1) * CONV_CW)
        acc = u_sc[off:off + tb, cs] * w_ref[0:1, cs]
        for j in range(1, CONV_K):
            acc = acc + u_sc[off + j:off + j + tb, cs] * w_ref[j:j + 1, cs]
        y_sc[:, cs] = acc + cb_ref[:, cs]
    y = y_sc[...]
    mu = jnp.mean(y, axis=-1, keepdims=True)
    yc = y - mu
    y = yc * lax.rsqrt(jnp.mean(yc * yc, axis=-1, keepdims=True) + NORM_EPS)
    y = y * lg_ref[...] + lb_ref[...]
    y = y * jax.nn.sigmoid(y)
    _group_rmsnorm_store(y, gg_ref, o_ref)


def _conv_prompt_kernel(cv_ref, cg_ref, pv_ref, pg_ref, w_ref, cb_ref, lg_ref, lb_ref, gg_ref,
                        o_ref, st_ref, u_sc, y_sc):
    u = cv_ref[...] * jax.nn.sigmoid(cg_ref[...])
    hist = pv_ref[...] * jax.nn.sigmoid(pg_ref[...])
    hist = jnp.where(pl.program_id(1) > 0, hist, 0.0)
    _conv_tail(u, hist, w_ref, cb_ref, lg_ref, lb_ref, gg_ref, o_ref, st_ref, u_sc, y_sc, CONV_TB)


def _conv_sample_kernel(cv_ref, cg_ref, past_ref, w_ref, cb_ref, lg_ref, lb_ref, gg_ref,
                        o_ref, st_ref, u_sc, y_sc, *, tb):
    u = cv_ref[...] * jax.nn.sigmoid(cg_ref[...])
    _conv_tail(u, past_ref[0], w_ref, cb_ref, lg_ref, lb_ref, gg_ref, o_ref, st_ref, u_sc, y_sc, tb)


def _conv_param_specs(nidx):
    zero = (lambda b, t: (0, 0)) if nidx == 2 else (lambda b: (0, 0))
    return [pl.BlockSpec((CONV_K, CONV_WIDTH), zero)] + [pl.BlockSpec((1, CONV_WIDTH), zero)] * 4


def _conv_prompt(z, params, batch, seq):
    ntb = seq // CONV_TB
    per = CONV_TB // CONV_HALO
    cv_blk, cg_blk = 3 * ATT_WIDTH // CONV_WIDTH, 3 * ATT_WIDTH // CONV_WIDTH + 1

    def cur(col):
        return pl.BlockSpec((CONV_TB, CONV_WIDTH), lambda b, t: (b * ntb + t, col))

    def prev(col):
        return pl.BlockSpec((CONV_HALO, CONV_WIDTH),
                            lambda b, t: (jnp.maximum((b * ntb + t) * per - 1, 0), col))

    return pl.pallas_call(
        _conv_prompt_kernel,
        grid=(batch, ntb),
        in_specs=[cur(cv_blk), cur(cg_blk), prev(cv_blk), prev(cg_blk)] + _conv_param_specs(2),
        out_specs=[pl.BlockSpec((CONV_TB, CONV_WIDTH), lambda b, t: (b * ntb + t, 0)),
                   pl.BlockSpec((1, CONV_HALO, CONV_WIDTH), lambda b, t: (b, 0, 0))],
        out_shape=[jax.ShapeDtypeStruct((batch * seq, CONV_WIDTH), BF16),
                   jax.ShapeDtypeStruct((batch, CONV_HALO, CONV_WIDTH), F32)],
        scratch_shapes=[pltpu.VMEM((CONV_HALO + CONV_TB, CONV_WIDTH), F32),
                        pltpu.VMEM((CONV_TB, CONV_WIDTH), F32)],
        compiler_params=_cparams(("arbitrary", "arbitrary"), 48),
        name="conv_prompt",
    )(z, z, z, z, *params)


def _conv_sample(z, row0, past, params, batch, new):
    assert row0 % new == 0 and new >= CONV_HALO
    blk0 = row0 // new
    cv_blk, cg_blk = 3 * ATT_WIDTH // CONV_WIDTH, 3 * ATT_WIDTH // CONV_WIDTH + 1
    return pl.pallas_call(
        functools.partial(_conv_sample_kernel, tb=new),
        grid=(batch,),
        in_specs=[pl.BlockSpec((new, CONV_WIDTH), lambda b: (blk0 + b, cv_blk)),
                  pl.BlockSpec((new, CONV_WIDTH), lambda b: (blk0 + b, cg_blk)),
                  pl.BlockSpec((1, CONV_HALO, CONV_WIDTH), lambda b: (b, 0, 0))] + _conv_param_specs(1),
        out_specs=[pl.BlockSpec((new, CONV_WIDTH), lambda b: (b, 0)),
                   pl.BlockSpec((1, CONV_HALO, CONV_WIDTH), lambda b: (b, 0, 0))],
        out_shape=[jax.ShapeDtypeStruct((batch * new, CONV_WIDTH), BF16),
                   jax.ShapeDtypeStruct((batch, CONV_HALO, CONV_WIDTH), F32)],
        scratch_shapes=[pltpu.VMEM((CONV_HALO + new, CONV_WIDTH), F32),
                        pltpu.VMEM((new, CONV_WIDTH), F32)],
        compiler_params=_cparams(("arbitrary",), 32),
        name="conv_sample",
    )(z, z, past, *params)


def _memattn_kernel(q_ref, mk_ref, mv_ref, o_ref):
    mk = mk_ref[0].astype(BF16)
    mv = mv_ref[0].astype(BF16)
    scale = HEAD_DIM ** -0.5
    for h in range(MEM_HEADS):
        hs = slice(h * HEAD_DIM, (h + 1) * HEAD_DIM)
        qh = q_ref[:, hs].astype(BF16)
        s = lax.dot_general(qh, mk[:, hs], (((1,), (1,)), ((), ())),
                            preferred_element_type=F32) * scale
        m = jnp.max(s, axis=-1, keepdims=True)
        p = jnp.exp(s - m)
        l = jnp.sum(p, axis=-1, keepdims=True)
        o = jnp.dot(p.astype(BF16), mv[:, hs], preferred_element_type=F32)
        o_ref[:, hs] = (o / l).astype(o_ref.dtype)


def _memattn(q, row0, mk, mv, k_col, v_col, batch, rows_per_batch, tb, name):
    nb = rows_per_batch // tb
    blk0 = row0 // tb
    return pl.pallas_call(
        _memattn_kernel,
        grid=(batch, nb),
        in_specs=[pl.BlockSpec((tb, MEM_WIDTH), lambda b, t: (blk0 + b * nb + t, 0)),
                  pl.BlockSpec((1, MEM_TOKENS, MEM_WIDTH), lambda b, t: (b, 0, k_col)),
                  pl.BlockSpec((1, MEM_TOKENS, MEM_WIDTH), lambda b, t: (b, 0, v_col))],
        out_specs=pl.BlockSpec((tb, MEM_WIDTH), lambda b, t: (b * nb + t, 0)),
        out_shape=jax.ShapeDtypeStruct((batch * rows_per_batch, MEM_WIDTH), BF16),
        compiler_params=_cparams(("arbitrary", "arbitrary"), 32),
        name=name,
    )(q, mk, mv)


def _router_kernel(h_ref, w_ref, b_ref, idx_ref, gate_ref):
    logits = jnp.dot(h_ref[...], w_ref[...], preferred_element_type=F32,
                     precision=lax.Precision.HIGHEST) + b_ref[...]
    lane = lax.broadcasted_iota(I32, logits.shape, 1)
    logits = jnp.where(lane < N_EXPERTS, logits, -jnp.inf)
    vals, idxs = [], []
    for _ in range(TOP_K):
        m = jnp.max(logits, axis=-1, keepdims=True)
        idx = jnp.min(jnp.where(logits == m, lane, LANES), axis=-1, keepdims=True)
        vals.append(m)
        idxs.append(idx)
        logits = jnp.where(lane == idx, -jnp.inf, logits)
    es = [jnp.exp(v - vals[0]) for v in vals]
    tot = es[0]
    for e in es[1:]:
        tot = tot + e
    idx_out = jnp.zeros(logits.shape, I32)
    gate_out = jnp.zeros(logits.shape, F32)
    for k in range(TOP_K):
        idx_out = jnp.where(lane == k, idxs[k], idx_out)
        gate_out = jnp.where(lane == k, es[k] / tot, gate_out)
    idx_ref[...] = idx_out
    gate_ref[...] = gate_out


def _router(h, w_router, b_router):
    rows, d = h.shape
    tm = NORM_ROWS
    wp = jnp.zeros((d, LANES), F32).at[:, :N_EXPERTS].set(w_router)
    bp = jnp.zeros((1, LANES), F32).at[0, :N_EXPERTS].set(b_router)
    return pl.pallas_call(
        _router_kernel,
        grid=(rows // tm,),
        in_specs=[pl.BlockSpec((tm, d), lambda i: (i, 0)),
                  pl.BlockSpec((d, LANES), lambda i: (0, 0)),
                  pl.BlockSpec((1, LANES), lambda i: (0, 0))],
        out_specs=[pl.BlockSpec((tm, LANES), lambda i: (i, 0)),
                   pl.BlockSpec((tm, LANES), lambda i: (i, 0))],
        out_shape=[jax.ShapeDtypeStruct((rows, LANES), I32),
                   jax.ShapeDtypeStruct((rows, LANES), F32)],
        compiler_params=_cparams(("arbitrary",), 40),
        name="router",
    )(h, wp, bp)


def _moe_tiles(n_assign):
    s_max = n_assign // MOE_NMAX + N_EXPERTS
    return s_max, s_max * MOE_NMAX


def _route(top_idx):
    n = top_idx.size
    s_max, rows_tot = _moe_tiles(n)
    e_flat = top_idx.reshape(-1)
    iota = jnp.arange(n, dtype=I32)
    _, order = lax.sort((e_flat, iota), num_keys=1, is_stable=True)
    _, rank_all = lax.sort((order, iota), num_keys=1)
    counts = jnp.sum((e_flat[:, None] == jnp.arange(N_EXPERTS, dtype=I32)[None, :]).astype(I32), axis=0)
    start = jnp.cumsum(counts) - counts
    n_st = (counts + MOE_NMAX - 1) // MOE_NMAX
    share = (counts + jnp.maximum(n_st, 1) - 1) // jnp.maximum(n_st, 1)
    q = jnp.maximum((share + MOE_R - 1) // MOE_R * MOE_R, MOE_R)
    st_end = jnp.cumsum(n_st)
    st_start = st_end - n_st
    n_live = st_end[-1]
    s_ids = jnp.arange(s_max, dtype=I32)
    st_e = jnp.minimum(jnp.searchsorted(st_end, s_ids, side="right"), N_EXPERTS - 1).astype(I32)
    t_in_e = s_ids - st_start[st_e]
    st_rows = jnp.clip(counts[st_e] - t_in_e * q[st_e], 0, q[st_e])
    st_rows = jnp.where(s_ids < n_live, st_rows, 0).astype(I32)
    st_src = (start[st_e] + t_in_e * q[st_e]).astype(I32)
    rank = rank_all - start[e_flat]
    pos = ((st_start[e_flat] + rank // q[e_flat]) * MOE_NMAX + rank % q[e_flat]).astype(I32)
    return dict(st_e=st_e, st_rows=st_rows, st_src=st_src, n_live=n_live.reshape(1).astype(I32),
                tok_sorted=(order // TOP_K).astype(I32), pos=pos)


def _gather_kernel(st_n, st_src, nblk_ref, tok_sorted, h_hbm, o_ref, buf, sem):
    i = pl.program_id(0)
    per = MOE_NMAX // GATHER_ROWS
    nblk = nblk_ref[0]
    slot = i % 2
    n_assign = tok_sorted.shape[0]

    def live(blk):
        return st_n[blk // per] > (blk % per) * GATHER_ROWS

    def row_copy(tok, k, s):
        return pltpu.make_async_copy(h_hbm.at[pl.ds(tok, 1), :], buf.at[s, pl.ds(k, 1), :], sem.at[s])

    def issue(blk, s):
        base = st_src[blk // per] + (blk % per) * GATHER_ROWS

        def body(k, c):
            row_copy(tok_sorted[jnp.minimum(base + k, n_assign - 1)], k, s).start()
            return c
        lax.fori_loop(0, GATHER_ROWS, body, 0, unroll=DMA_UNROLL)

    @pl.when((i == 0) & live(0))
    def _():
        issue(0, 0)

    nxt = jnp.minimum(i + 1, nblk - 1)

    @pl.when((i + 1 < nblk) & live(nxt))
    def _():
        issue(nxt, 1 - slot)

    @pl.when(live(i))
    def _():
        def wait(k, c):
            row_copy(0, k, slot).wait()
            return c
        lax.fori_loop(0, GATHER_ROWS, wait, 0, unroll=DMA_UNROLL)
        o_ref[...] = buf[slot].astype(o_ref.dtype)


def _gather(h, route, rows_tot):
    d = h.shape[1]
    per = MOE_NMAX // GATHER_ROWS
    nblk = route["n_live"] * per
    return pl.pallas_call(
        _gather_kernel,
        grid_spec=pltpu.PrefetchScalarGridSpec(
            num_scalar_prefetch=4,
            grid=(nblk[0],),
            in_specs=[pl.BlockSpec(memory_space=pl.ANY)],
            out_specs=pl.BlockSpec((GATHER_ROWS, d), lambda i, *_: (i, 0)),
            scratch_shapes=[pltpu.VMEM((2, GATHER_ROWS, d), F32), pltpu.SemaphoreType.DMA((2,))]),
        out_shape=jax.ShapeDtypeStruct((rows_tot, d), BF16),
        compiler_params=_cparams(("arbitrary",), 32),
        name="moe_gather",
    )(route["st_rows"], route["st_src"], nblk, route["tok_sorted"], h)


def _row_passes(cnt):
    return (cnt + MOE_R - 1) // MOE_R


def _for_row_count(cnt, fn):
    npass = _row_passes(cnt)
    for k in range(1, MOE_NMAX // MOE_R + 1):
        pl.when(npass == k)(functools.partial(fn, k * MOE_R))


def _moe_up_kernel(st_e, st_n, xs_ref, wg_ref, wu_ref, bg_ref, bu_ref, o_ref):
    def compute(rows):
        x = xs_ref[0:rows, :]
        g = jnp.dot(x, wg_ref[0].astype(BF16), preferred_element_type=F32) + bg_ref[0]
        u = jnp.dot(x, wu_ref[0].astype(BF16), preferred_element_type=F32) + bu_ref[0]
        g = jnp.minimum(g, SWIGLU_LIMIT)
        u = jnp.clip(u, -SWIGLU_LIMIT, SWIGLU_LIMIT)
        o_ref[0:rows, :] = ((u + 1.0) * (g * jax.nn.sigmoid(SWIGLU_ALPHA * g))).astype(o_ref.dtype)

    _for_row_count(st_n[pl.program_id(0)], compute)


def _moe_down_kernel(st_e, st_n, a_ref, wd_ref, bd_ref, o_ref):
    def compute(rows):
        o_ref[0:rows, :] = jnp.dot(a_ref[0:rows, :], wd_ref[0].astype(BF16),
                                   preferred_element_type=F32) + bd_ref[0]

    _for_row_count(st_n[pl.program_id(0)], compute)


def _w_cols(off):
    return lambda s, j, st_e, st_n: (st_e[s], 0, off + j)


def _moe_up(xs, route, w_gate_up, b_gate_up):
    rows_tot, d = xs.shape
    nj = D_EXPERT // MOE_TN_UP
    bias = b_gate_up.reshape(N_EXPERTS, 1, 2 * D_EXPERT)
    return pl.pallas_call(
        _moe_up_kernel,
        grid_spec=pltpu.PrefetchScalarGridSpec(
            num_scalar_prefetch=2,
            grid=(route["n_live"][0], nj),
            in_specs=[pl.BlockSpec((MOE_NMAX, d), lambda s, j, st_e, st_n: (s, 0)),
                      pl.BlockSpec((1, d, MOE_TN_UP), _w_cols(0)),
                      pl.BlockSpec((1, d, MOE_TN_UP), _w_cols(nj)),
                      pl.BlockSpec((1, 1, MOE_TN_UP), _w_cols(0)),
                      pl.BlockSpec((1, 1, MOE_TN_UP), _w_cols(nj))],
            out_specs=pl.BlockSpec((MOE_NMAX, MOE_TN_UP), lambda s, j, st_e, st_n: (s, j))),
        out_shape=jax.ShapeDtypeStruct((rows_tot, D_EXPERT), BF16),
        compiler_params=_cparams(("arbitrary", "arbitrary"), 52),
        name="moe_up",
    )(route["st_e"], route["st_rows"], xs, w_gate_up, w_gate_up, bias, bias)


def _moe_down(act, route, w_down, b_down):
    rows_tot, f = act.shape
    d = w_down.shape[2]
    nj = d // MOE_TN_DOWN
    return pl.pallas_call(
        _moe_down_kernel,
        grid_spec=pltpu.PrefetchScalarGridSpec(
            num_scalar_prefetch=2,
            grid=(route["n_live"][0], nj),
            in_specs=[pl.BlockSpec((MOE_NMAX, f), lambda s, j, st_e, st_n: (s, 0)),
                      pl.BlockSpec((1, f, MOE_TN_DOWN), _w_cols(0)),
                      pl.BlockSpec((1, 1, MOE_TN_DOWN), _w_cols(0))],
            out_specs=pl.BlockSpec((MOE_NMAX, MOE_TN_DOWN), lambda s, j, st_e, st_n: (s, j))),
        out_shape=jax.ShapeDtypeStruct((rows_tot, d), F32),
        compiler_params=_cparams(("arbitrary", "arbitrary"), 52),
        name="moe_down",
    )(route["st_e"], route["st_rows"], act, w_down, b_down.reshape(N_EXPERTS, 1, d))


def _combine_kernel(pos_ref, posn_ref, ys_hbm, x_ref, gate_ref, g_ref, op_ref, os_ref, buf, sem,
                    *, n_blk, n_prompt_blk):
    i = pl.program_id(0)
    slot = i % 2
    n = COMBINE_TOK * TOP_K

    def row_copy(p, t, k, s):
        return pltpu.make_async_copy(ys_hbm.at[pl.ds(p, 1), :], buf.at[s, k, pl.ds(t, 1), :], sem.at[s])

    def issue(pos_smem, s):
        def body(t, c):
            for k in range(TOP_K):
                row_copy(pos_smem[t * TOP_K + k], t, k, s).start()
            return c
        lax.fori_loop(0, COMBINE_TOK, body, 0, unroll=DMA_UNROLL // TOP_K)

    @pl.when(i == 0)
    def _():
        issue(pos_ref, 0)

    @pl.when(i + 1 < n_blk)
    def _():
        issue(posn_ref, 1 - slot)

    def wait(t, c):
        for k in range(TOP_K):
            row_copy(0, t, k, slot).wait()
        return c

    lax.fori_loop(0, COMBINE_TOK, wait, 0, unroll=DMA_UNROLL // TOP_K)
    x = x_ref[...]
    for k in range(TOP_K):
        x = x + buf[slot, k] * gate_ref[:, k:k + 1]
    ms = jnp.mean(x * x, axis=-1, keepdims=True)
    y = x * lax.rsqrt(ms + NORM_EPS) * g_ref[...]

    @pl.when(i < n_prompt_blk)
    def _():
        op_ref[...] = y

    @pl.when(i >= n_prompt_blk)
    def _():
        os_ref[...] = y


def _combine(ys, pos, x, gates, g_final, rows_prompt):
    rows, d = x.shape
    n_blk = rows // COMBINE_TOK
    npb = rows_prompt // COMBINE_TOK
    nidx = COMBINE_TOK * TOP_K
    return pl.pallas_call(
        functools.partial(_combine_kernel, n_blk=n_blk, n_prompt_blk=npb),
        grid=(n_blk,),
        in_specs=[pl.BlockSpec((nidx,), lambda i: (i,), memory_space=pltpu.SMEM),
                  pl.BlockSpec((nidx,), lambda i: (jnp.minimum(i + 1, n_blk - 1),), memory_space=pltpu.SMEM),
                  pl.BlockSpec(memory_space=pl.ANY),
                  pl.BlockSpec((COMBINE_TOK, d), lambda i: (i, 0)),
                  pl.BlockSpec((COMBINE_TOK, LANES), lambda i: (i, 0)),
                  pl.BlockSpec((1, d), lambda i: (0, 0))],
        out_specs=[pl.BlockSpec((COMBINE_TOK, d), lambda i: (jnp.minimum(i, npb - 1), 0)),
                   pl.BlockSpec((COMBINE_TOK, d), lambda i: (jnp.maximum(i - npb, 0), 0))],
        out_shape=[jax.ShapeDtypeStruct((rows_prompt, d), F32),
                   jax.ShapeDtypeStruct((rows - rows_prompt, d), F32)],
        scratch_shapes=[pltpu.VMEM((2, TOP_K, COMBINE_TOK, d), F32), pltpu.SemaphoreType.DMA((2,))],
        compiler_params=_cparams(("arbitrary",), 32),
        name="moe_combine",
    )(pos, pos, ys, x, gates, g_final.reshape(1, d))


def _tail_rows(z, batch, seq, nrows, c0, c1):
    return jnp.stack([lax.slice(z, ((b + 1) * seq - nrows, c0), ((b + 1) * seq, c1)) for b in range(batch)])


def kernel(x_prompt, x_sample, cache_band_k, cache_band_v, state_conv, cache_mem_k, cache_mem_v, mem_prompt, norm_mix, w_in, rel_table, conv_w, conv_b, conv_ln_g, conv_ln_b, grp_g_att, grp_g_conv, w_out, norm_mem_q, norm_mem_kv, w_mem_q, w_mem_k, w_mem_v, w_mem_o, norm_ffn, w_router, b_router, w_gate_up, b_gate_up, w_down, b_down, norm_final):
    depth = norm_mix.shape[0]
    assert depth == 1
    bp, sp, d = x_prompt.shape
    bs, ss, _ = x_sample.shape
    tp, ts = bp * sp, bs * ss
    t_all = tp + ts
    assert d == D_MODEL and sp % QB == 0 and sp % CONV_TB == 0 and tp % MM_ROWS == 0
    assert sp >= BAND_WINDOW and ss == CHUNK and t_all % MM_ROWS == 0
    assert tp % COMBINE_TOK == 0 and ts % COMBINE_TOK == 0
    l = 0

    x = (x_prompt.reshape(tp, d), x_sample.reshape(ts, d))

    h = _rmsnorm(x, norm_mix[l], BF16)
    z = _matmul([h], w_in[l].astype(BF16), None, F32, 1024, "in_proj")
    w_rows = _rel_rows(rel_table[l])
    att_p = _attn_prompt(z, w_rows, grp_g_att[l], bp, sp)
    ck = cache_band_k[l].reshape(bs, -1, HEAD_DIM)
    cv = cache_band_v[l].reshape(bs, -1, HEAD_DIM)
    att_s = _attn_sample(z, tp, ck, cv, w_rows, grp_g_att[l], bs, ss)
    conv_params = (conv_w[l], conv_b[l].reshape(1, -1), conv_ln_g[l].reshape(1, -1),
                   conv_ln_b[l].reshape(1, -1), grp_g_conv[l].reshape(1, -1))
    cnv_p, cs_p = _conv_prompt(z, conv_params, bp, sp)
    past = jnp.pad(state_conv[l], ((0, 0), (CONV_HALO - (CONV_K - 1), 0), (0, 0)))
    cnv_s, cs_s = _conv_sample(z, tp, past, conv_params, bs, ss)
    x1 = _matmul([(att_p, att_s), (cnv_p, cnv_s)], w_out[l].astype(BF16), x, F32, 512, "out_proj",
                 tm=MM_ROWS // 2)

    hd = (ATT_HEADS, HEAD_DIM)
    bk_p = _tail_rows(z, bp, sp, BAND_WINDOW, ATT_WIDTH, 2 * ATT_WIDTH).reshape(1, bp, BAND_WINDOW, *hd)
    bv_p = _tail_rows(z, bp, sp, BAND_WINDOW, 2 * ATT_WIDTH, 3 * ATT_WIDTH).reshape(1, bp, BAND_WINDOW, *hd)
    bk_s = lax.slice(z, (tp, ATT_WIDTH), (t_all, 2 * ATT_WIDTH)).reshape(1, bs, ss, *hd)
    bv_s = lax.slice(z, (tp, 2 * ATT_WIDTH), (t_all, 3 * ATT_WIDTH)).reshape(1, bs, ss, *hd)
    keep = CONV_HALO - (CONV_K - 1)
    cs_p = cs_p[:, keep:][None]
    cs_s = cs_s[:, keep:][None]

    nm = mem_prompt.shape[1]
    mem_n = _rmsnorm(mem_prompt.reshape(bp * nm, d), norm_mem_kv[l], BF16)
    w_mkv = jnp.concatenate([w_mem_k[l], w_mem_v[l]], axis=1).astype(BF16)
    mkv = _matmul([mem_n], w_mkv, None, F32, 2 * MEM_WIDTH, "mem_kv").reshape(bp, nm, 2 * MEM_WIDTH)
    mk_p = mkv[:, :, :MEM_WIDTH].reshape(1, bp, nm, MEM_HEADS, HEAD_DIM)
    mv_p = mkv[:, :, MEM_WIDTH:].reshape(1, bp, nm, MEM_HEADS, HEAD_DIM)
    hq = _rmsnorm(x1, norm_mem_q[l], BF16)
    q = _matmul([hq], w_mem_q[l].astype(BF16), None, F32, MEM_WIDTH, "mem_q")
    mo_p = _memattn(q, 0, mkv, mkv, 0, 1, bp, sp, 512, "memattn_prompt")
    cmk = cache_mem_k[l].reshape(bs, nm, MEM_WIDTH)
    cmv = cache_mem_v[l].reshape(bs, nm, MEM_WIDTH)
    mo_s = _memattn(q, tp, cmk, cmv, 0, 0, bs, ss, ss, "memattn_sample")
    x2 = _matmul([(mo_p, mo_s)], w_mem_o[l].astype(BF16), x1, F32, 512, "mem_o")

    hf = _rmsnorm(x2, norm_ffn[l], F32)
    top_idx, gates = _router(hf, w_router[l], b_router[l])
    route = _route(top_idx[:, :TOP_K])
    _, rows_tot = _moe_tiles(t_all * TOP_K)
    xs = _gather(hf, route, rows_tot)
    act = _moe_up(xs, route, w_gate_up[l], b_gate_up[l])
    ys = _moe_down(act, route, w_down[l], b_down[l])
    y_p, y_s = _combine(ys, route["pos"], x2, gates, norm_final, tp)

    return (y_p.reshape(bp, sp, d), y_s.reshape(bs, ss, d), bk_p, bv_p, cs_p, mk_p, mv_p, bk_s, bv_s, cs_s)
```

```python
import functools

import jax
import jax.numpy as jnp
from jax import lax
from jax.experimental import pallas as pl
from jax.experimental.pallas import tpu as pltpu

F32 = jnp.float32
BF16 = jnp.bfloat16
I32 = jnp.int32

D_MODEL = 4096
CHUNK = 64
LEFT_CHUNKS = 8
BAND_WINDOW = LEFT_CHUNKS * CHUNK
HEAD_DIM = 128
ATT_WIDTH = 2048
CONV_WIDTH = 2048
ATT_HEADS = 16
IN_WIDTH = 3 * ATT_WIDTH + 2 * CONV_WIDTH
MAX_REL = 256
CONV_K = 31
MEM_TOKENS = 256
MEM_HEADS = 4
MEM_WIDTH = 512
N_EXPERTS = 32
TOP_K = 4
D_EXPERT = 4096
SWIGLU_LIMIT = 7.0
SWIGLU_ALPHA = 1.702
NORM_EPS = 1e-5
NEG_INF = -1e30

LANES = 128
SUBLANES = 8

NORM_ROWS = 512
MM_ROWS = 1024
ATTN_HG = 8
QB = 256
KWIN = QB + BAND_WINDOW
ROLL_W = 1024
SKEYS = 640
CONV_TB = 256
CONV_HALO = 32
CONV_CW = 512
MOE_R = 128
MOE_NMAX = 1280
MOE_TN_UP = 256
MOE_TN_DOWN = 512
GATHER_ROWS = 256
COMBINE_TOK = 64
DMA_UNROLL = 8


def _cparams(sem, vmem_mib):
    return pltpu.CompilerParams(dimension_semantics=sem, vmem_limit_bytes=vmem_mib * 2**20)


def _pair_specs(pair, tm, width, col_map=None):
    first, second = pair
    assert first.shape[0] % tm == 0 and second.shape[0] % tm == 0
    nf = first.shape[0] // tm
    col = col_map if col_map is not None else (lambda *ij: 0)
    specs = [pl.BlockSpec((tm, width), lambda *ij: (jnp.minimum(ij[0], nf - 1), col(*ij))),
             pl.BlockSpec((tm, width), lambda *ij: (jnp.maximum(ij[0] - nf, 0), col(*ij)))]
    return specs, nf


def _pair_value(refs, nf):
    return jnp.where(pl.program_id(0) < nf, refs[0][...], refs[1][...])


def _rmsnorm_kernel(*refs, nf):
    g_ref, o_ref = refs[-2], refs[-1]
    x = refs[0][...] if nf is None else _pair_value(refs[:2], nf)
    ms = jnp.mean(x * x, axis=-1, keepdims=True)
    o_ref[...] = (x * lax.rsqrt(ms + NORM_EPS) * g_ref[...]).astype(o_ref.dtype)


def _rmsnorm(x, g, out_dtype):
    if isinstance(x, tuple):
        rows, d = x[0].shape[0] + x[1].shape[0], x[0].shape[1]
        tm = NORM_ROWS // 2
        x_specs, nf = _pair_specs(x, tm, d)
        xs = list(x)
    else:
        rows, d = x.shape
        tm = min(NORM_ROWS, rows)
        x_specs, nf, xs = [pl.BlockSpec((tm, d), lambda i: (i, 0))], None, [x]
    return pl.pallas_call(
        functools.partial(_rmsnorm_kernel, nf=nf),
        grid=(rows // tm,),
        in_specs=x_specs + [pl.BlockSpec((1, d), lambda i: (0, 0))],
        out_specs=pl.BlockSpec((tm, d), lambda i: (i, 0)),
        out_shape=jax.ShapeDtypeStruct((rows, d), out_dtype),
        compiler_params=_cparams(("arbitrary",), 48),
        name="rmsnorm",
    )(*xs, g.reshape(1, d))


def _mm_kernel(*refs, layout, nf):
    a_counts, res_count = layout[:-1], layout[-1]
    pos = 0
    a_vals = []
    for c in a_counts:
        a_vals.append(refs[pos][...] if c == 1 else _pair_value(refs[pos:pos + 2], nf))
        pos += c
    b_refs = refs[pos:pos + len(a_counts)]
    pos += len(a_counts)
    acc = jnp.dot(a_vals[0], b_refs[0][...], preferred_element_type=F32)
    for k in range(1, len(a_vals)):
        acc = acc + jnp.dot(a_vals[k], b_refs[k][...], preferred_element_type=F32)
    if res_count:
        acc = acc + (refs[pos][...] if res_count == 1 else _pair_value(refs[pos:pos + 2], nf))
    refs[-1][...] = acc.astype(refs[-1].dtype)


def _rows_of(x):
    return x[0].shape[0] + x[1].shape[0] if isinstance(x, tuple) else x.shape[0]


def _matmul(a_parts, w, res, out_dtype, tn, name, tm=MM_ROWS):
    m = _rows_of(a_parts[0])
    kk = (a_parts[0][0] if isinstance(a_parts[0], tuple) else a_parts[0]).shape[1]
    n = w.shape[1]
    tm = min(tm, m)
    in_specs, args, layout, nf = [], [], [], None
    for a in a_parts:
        if isinstance(a, tuple):
            specs, nf = _pair_specs(a, tm, kk)
            in_specs += specs
            args += list(a)
            layout.append(2)
        else:
            in_specs.append(pl.BlockSpec((tm, kk), lambda i, j: (i, 0)))
            args.append(a)
            layout.append(1)
    in_specs += [pl.BlockSpec((kk, tn), functools.partial(lambda i, j, p: (p, j), p=p))
                 for p in range(len(a_parts))]
    args += [w] * len(a_parts)
    if isinstance(res, tuple):
        specs, nf = _pair_specs(res, tm, tn, col_map=lambda i, j: j)
        in_specs += specs
        args += list(res)
        layout.append(2)
    elif res is not None:
        in_specs.append(pl.BlockSpec((tm, tn), lambda i, j: (i, j)))
        args.append(res)
        layout.append(1)
    else:
        layout.append(0)
    return pl.pallas_call(
        functools.partial(_mm_kernel, layout=tuple(layout), nf=nf),
        grid=(m // tm, n // tn),
        in_specs=in_specs,
        out_specs=pl.BlockSpec((tm, tn), lambda i, j: (i, j)),
        out_shape=jax.ShapeDtypeStruct((m, n), out_dtype),
        compiler_params=_cparams(("arbitrary", "arbitrary"), 56),
        name=name,
    )(*args)


def _rel_rows(table):
    m = jnp.arange(ROLL_W)
    m = jnp.where(m < KWIN, m, m - ROLL_W)
    idx = jnp.clip(BAND_WINDOW - m, -MAX_REL, MAX_REL) + MAX_REL
    return table[:, idx].astype(F32)


def _toeplitz_bias(w_row, rows):
    wb = jnp.broadcast_to(w_row, (rows, ROLL_W))
    return pltpu.roll(wb, 0, 1, stride=1, stride_axis=0)


def _group_rmsnorm_store(o, g_ref, o_ref):
    ms = jnp.mean(o * o, axis=-1, keepdims=True)
    o_ref[...] = (o * lax.rsqrt(ms + NORM_EPS) * g_ref[...]).astype(o_ref.dtype)


def _attn_prompt_kernel(w_ref, q_ref, k0_ref, k1_ref, k2_ref, v0_ref, v1_ref, v2_ref, g_ref,
                        o_ref, bias_sc, o_sc):
    i = pl.program_id(1)
    hg = pl.program_id(2)
    first = (pl.program_id(0) == 0) & (i == 0) & (hg == 0)

    @pl.when(first)
    def _():
        r = lax.broadcasted_iota(I32, (QB, KWIN), 0) // CHUNK
        c = lax.broadcasted_iota(I32, (QB, KWIN), 1) // CHUNK
        band = (c >= r) & (c <= r + LEFT_CHUNKS)
        for h in range(ATT_HEADS):
            t = _toeplitz_bias(w_ref[h:h + 1, :], QB)
            bias_sc[h] = jnp.where(band, t[:, :KWIN], NEG_INF)

    kk = jnp.concatenate([r[...].astype(BF16) for r in (k0_ref, k1_ref, k2_ref)], axis=0)
    vv = jnp.concatenate([r[...].astype(BF16) for r in (v0_ref, v1_ref, v2_ref)], axis=0)
    col = lax.broadcasted_iota(I32, (QB, KWIN), 1)
    in_seq = col >= (BAND_WINDOW // QB - i) * QB
    scale = HEAD_DIM ** -0.5
    for hh in range(ATTN_HG):
        hs = slice(hh * HEAD_DIM, (hh + 1) * HEAD_DIM)
        qh = q_ref[:, hs].astype(BF16)
        s = lax.dot_general(qh, kk[:, hs], (((1,), (1,)), ((), ())),
                            preferred_element_type=F32) * scale + bias_sc[hg * ATTN_HG + hh]
        s = jnp.where(in_seq, s, NEG_INF)
        m = jnp.max(s, axis=-1, keepdims=True)
        p = jnp.exp(s - m)
        l = jnp.sum(p, axis=-1, keepdims=True)
        o = jnp.dot(p.astype(BF16), vv[:, hs], preferred_element_type=F32)
        o_sc[hg, :, hs] = o / l

    @pl.when(hg == ATT_HEADS // ATTN_HG - 1)
    def _():
        o = jnp.concatenate([o_sc[g] for g in range(ATT_HEADS // ATTN_HG)], axis=1)
        _group_rmsnorm_store(o, g_ref, o_ref)


def _attn_prompt(z, w_rows, g_att, batch, seq):
    nqb = seq // QB
    back = BAND_WINDOW // QB
    gw = ATTN_HG * HEAD_DIM
    ngrp = ATT_HEADS // ATTN_HG

    def kv_spec(col0, d):
        return pl.BlockSpec((QB, gw),
                            lambda b, i, g: (b * nqb + jnp.maximum(i - d, 0), col0 + g))

    kc, vc = ATT_WIDTH // gw, 2 * ATT_WIDTH // gw
    return pl.pallas_call(
        _attn_prompt_kernel,
        grid=(batch, nqb, ngrp),
        in_specs=[pl.BlockSpec((ATT_HEADS, ROLL_W), lambda b, i, g: (0, 0)),
                  pl.BlockSpec((QB, gw), lambda b, i, g: (b * nqb + i, g)),
                  kv_spec(kc, back), kv_spec(kc, back - 1), kv_spec(kc, 0),
                  kv_spec(vc, back), kv_spec(vc, back - 1), kv_spec(vc, 0),
                  pl.BlockSpec((1, ATT_WIDTH), lambda b, i, g: (0, 0))],
        out_specs=pl.BlockSpec((QB, ATT_WIDTH), lambda b, i, g: (b * nqb + i, 0)),
        out_shape=jax.ShapeDtypeStruct((batch * seq, ATT_WIDTH), BF16),
        scratch_shapes=[pltpu.VMEM((ATT_HEADS, QB, KWIN), F32),
                        pltpu.VMEM((ngrp, QB, gw), F32)],
        compiler_params=_cparams(("arbitrary", "arbitrary", "arbitrary"), 48),
        name="attn_prompt",
    )(w_rows, z, z, z, z, z, z, z, g_att.reshape(1, ATT_WIDTH))


def _attn_sample_kernel(w_ref, q_ref, kn_ref, vn_ref, kc_ref, vc_ref, g_ref, o_ref, bias_sc, o_sc,
                        *, past, new):
    @pl.when(pl.program_id(0) == 0)
    def _():
        c = lax.broadcasted_iota(I32, (new, SKEYS), 1)
        for h in range(ATT_HEADS):
            t = _toeplitz_bias(w_ref[h:h + 1, :], new)
            bias_sc[h] = jnp.where(c < past + new, t[:, :SKEYS], NEG_INF)

    pad = jnp.zeros((SKEYS - past - new, HEAD_DIM), BF16)
    scale = HEAD_DIM ** -0.5
    for h in range(ATT_HEADS):
        hs = slice(h * HEAD_DIM, (h + 1) * HEAD_DIM)
        head_rows = pl.ds(h, past, stride=ATT_HEADS)
        kh = jnp.concatenate([kc_ref[0, head_rows, :].astype(BF16), kn_ref[:, hs].astype(BF16), pad], axis=0)
        vh = jnp.concatenate([vc_ref[0, head_rows, :].astype(BF16), vn_ref[:, hs].astype(BF16), pad], axis=0)
        qh = q_ref[:, hs].astype(BF16)
        s = lax.dot_general(qh, kh, (((1,), (1,)), ((), ())),
                            preferred_element_type=F32) * scale + bias_sc[h]
        m = jnp.max(s, axis=-1, keepdims=True)
        p = jnp.exp(s - m)
        l = jnp.sum(p, axis=-1, keepdims=True)
        o = jnp.dot(p.astype(BF16), vh, preferred_element_type=F32)
        o_sc[:, hs] = o / l
    _group_rmsnorm_store(o_sc[...], g_ref, o_ref)


def _attn_sample(z, row0, cache_k, cache_v, w_rows, g_att, batch, new):
    past = cache_k.shape[1] // ATT_HEADS
    assert past == BAND_WINDOW and past + new <= SKEYS and row0 % new == 0
    blk0 = row0 // new
    return pl.pallas_call(
        functools.partial(_attn_sample_kernel, past=past, new=new),
        grid=(batch,),
        in_specs=[pl.BlockSpec((ATT_HEADS, ROLL_W), lambda b: (0, 0)),
                  pl.BlockSpec((new, ATT_WIDTH), lambda b: (blk0 + b, 0)),
                  pl.BlockSpec((new, ATT_WIDTH), lambda b: (blk0 + b, 1)),
                  pl.BlockSpec((new, ATT_WIDTH), lambda b: (blk0 + b, 2)),
                  pl.BlockSpec((1, past * ATT_HEADS, HEAD_DIM), lambda b: (b, 0, 0)),
                  pl.BlockSpec((1, past * ATT_HEADS, HEAD_DIM), lambda b: (b, 0, 0)),
                  pl.BlockSpec((1, ATT_WIDTH), lambda b: (0, 0))],
        out_specs=pl.BlockSpec((new, ATT_WIDTH), lambda b: (b, 0)),
        out_shape=jax.ShapeDtypeStruct((batch * new, ATT_WIDTH), BF16),
        scratch_shapes=[pltpu.VMEM((ATT_HEADS, new, SKEYS), F32),
                        pltpu.VMEM((new, ATT_WIDTH), F32)],
        compiler_params=_cparams(("arbitrary",), 48),
        name="attn_sample",
    )(w_rows, z, z, z, cache_k, cache_v, g_att.reshape(1, ATT_WIDTH))


def _conv_tail(u, hist, w_ref, cb_ref, lg_ref, lb_ref, gg_ref, o_ref, st_ref, u_sc, y_sc, s_sc, tb):
    u_sc[0:CONV_HALO, :] = hist
    u_sc[CONV_HALO:CONV_HALO + tb, :] = u
    u_sc[CONV_HALO + tb:CONV_HALO + tb + SUBLANES, :] = jnp.zeros((SUBLANES, CONV_WIDTH), F32)
    st_ref[0] = u[tb - CONV_HALO:, :]
    off = CONV_HALO - (CONV_K - 1)
    srows = tb + SUBLANES
    for cc in range(CONV_WIDTH // CONV_CW):
        cs = slice(cc * CONV_CW, (cc + 1) * CONV_CW)
        for p in range(SUBLANES):
            acc = None
            for j in range(CONV_K):
                if (off + j) % SUBLANES != p:
                    continue
                a = off + j - p
                term = u_sc[a:a + srows, cs] * w_ref[j:j + 1, cs]
                acc = term if acc is None else acc + term
            s_sc[p] = acc
        y = s_sc[0, 0:tb, :]
        for p in range(1, SUBLANES):
            y = y + s_sc[p, p:p + tb, :]
        y_sc[:, cs] = y + cb_ref[:, cs]
    y = y_sc[...]
    mu = jnp.mean(y, axis=-1, keepdims=True)
    yc = y - mu
    y = yc * lax.rsqrt(jnp.mean(yc * yc, axis=-1, keepdims=True) + NORM_EPS)
    y = y * lg_ref[...] + lb_ref[...]
    y = y * jax.nn.sigmoid(y)
    _group_rmsnorm_store(y, gg_ref, o_ref)


def _conv_prompt_kernel(cv_ref, cg_ref, pv_ref, pg_ref, w_ref, cb_ref, lg_ref, lb_ref, gg_ref,
                        o_ref, st_ref, u_sc, y_sc, s_sc):
    u = cv_ref[...] * jax.nn.sigmoid(cg_ref[...])
    hist = pv_ref[...] * jax.nn.sigmoid(pg_ref[...])
    hist = jnp.where(pl.program_id(1) > 0, hist, 0.0)
    _conv_tail(u, hist, w_ref, cb_ref, lg_ref, lb_ref, gg_ref, o_ref, st_ref, u_sc, y_sc, s_sc, CONV_TB)


def _conv_sample_kernel(cv_ref, cg_ref, past_ref, w_ref, cb_ref, lg_ref, lb_ref, gg_ref,
                        o_ref, st_ref, u_sc, y_sc, s_sc, *, tb):
    u = cv_ref[...] * jax.nn.sigmoid(cg_ref[...])
    _conv_tail(u, past_ref[0], w_ref, cb_ref, lg_ref, lb_ref, gg_ref, o_ref, st_ref, u_sc, y_sc, s_sc, tb)


def _conv_scratch(tb):
    return [pltpu.VMEM((CONV_HALO + tb + SUBLANES, CONV_WIDTH), F32),
            pltpu.VMEM((tb, CONV_WIDTH), F32),
            pltpu.VMEM((SUBLANES, tb + SUBLANES, CONV_CW), F32)]


def _conv_param_specs(nidx):
    zero = (lambda b, t: (0, 0)) if nidx == 2 else (lambda b: (0, 0))
    return [pl.BlockSpec((CONV_K, CONV_WIDTH), zero)] + [pl.BlockSpec((1, CONV_WIDTH), zero)] * 4


def _conv_prompt(z, params, batch, seq):
    ntb = seq // CONV_TB
    per = CONV_TB // CONV_HALO
    cv_blk, cg_blk = 3 * ATT_WIDTH // CONV_WIDTH, 3 * ATT_WIDTH // CONV_WIDTH + 1

    def cur(col):
        return pl.BlockSpec((CONV_TB, CONV_WIDTH), lambda b, t: (b * ntb + t, col))

    def prev(col):
        return pl.BlockSpec((CONV_HALO, CONV_WIDTH),
                            lambda b, t: (jnp.maximum((b * ntb + t) * per - 1, 0), col))

    return pl.pallas_call(
        _conv_prompt_kernel,
        grid=(batch, ntb),
        in_specs=[cur(cv_blk), cur(cg_blk), prev(cv_blk), prev(cg_blk)] + _conv_param_specs(2),
        out_specs=[pl.BlockSpec((CONV_TB, CONV_WIDTH), lambda b, t: (b * ntb + t, 0)),
                   pl.BlockSpec((1, CONV_HALO, CONV_WIDTH), lambda b, t: (b, 0, 0))],
        out_shape=[jax.ShapeDtypeStruct((batch * seq, CONV_WIDTH), BF16),
                   jax.ShapeDtypeStruct((batch, CONV_HALO, CONV_WIDTH), F32)],
        scratch_shapes=_conv_scratch(CONV_TB),
        compiler_params=_cparams(("arbitrary", "arbitrary"), 48),
        name="conv_prompt",
    )(z, z, z, z, *params)


def _conv_sample(z, row0, past, params, batch, new):
    assert row0 % new == 0 and new >= CONV_HALO
    blk0 = row0 // new
    cv_blk, cg_blk = 3 * ATT_WIDTH // CONV_WIDTH, 3 * ATT_WIDTH // CONV_WIDTH + 1
    return pl.pallas_call(
        functools.partial(_conv_sample_kernel, tb=new),
        grid=(batch,),
        in_specs=[pl.BlockSpec((new, CONV_WIDTH), lambda b: (blk0 + b, cv_blk)),
                  pl.BlockSpec((new, CONV_WIDTH), lambda b: (blk0 + b, cg_blk)),
                  pl.BlockSpec((1, CONV_HALO, CONV_WIDTH), lambda b: (b, 0, 0))] + _conv_param_specs(1),
        out_specs=[pl.BlockSpec((new, CONV_WIDTH), lambda b: (b, 0)),
                   pl.BlockSpec((1, CONV_HALO, CONV_WIDTH), lambda b: (b, 0, 0))],
        out_shape=[jax.ShapeDtypeStruct((batch * new, CONV_WIDTH), BF16),
                   jax.ShapeDtypeStruct((batch, CONV_HALO, CONV_WIDTH), F32)],
        scratch_shapes=_conv_scratch(new),
        compiler_params=_cparams(("arbitrary",), 32),
        name="conv_sample",
    )(z, z, past, *params)


def _memattn_kernel(q_ref, mk_ref, mv_ref, o_ref):
    mk = mk_ref[0].astype(BF16)
    mv = mv_ref[0].astype(BF16)
    scale = HEAD_DIM ** -0.5
    for h in range(MEM_HEADS):
        hs = slice(h * HEAD_DIM, (h + 1) * HEAD_DIM)
        qh = q_ref[:, hs].astype(BF16)
        s = lax.dot_general(qh, mk[:, hs], (((1,), (1,)), ((), ())),
                            preferred_element_type=F32) * scale
        m = jnp.max(s, axis=-1, keepdims=True)
        p = jnp.exp(s - m)
        l = jnp.sum(p, axis=-1, keepdims=True)
        o = jnp.dot(p.astype(BF16), mv[:, hs], preferred_element_type=F32)
        o_ref[:, hs] = (o / l).astype(o_ref.dtype)


def _memattn(q, row0, mk, mv, k_col, v_col, batch, rows_per_batch, tb, name):
    nb = rows_per_batch // tb
    blk0 = row0 // tb
    return pl.pallas_call(
        _memattn_kernel,
        grid=(batch, nb),
        in_specs=[pl.BlockSpec((tb, MEM_WIDTH), lambda b, t: (blk0 + b * nb + t, 0)),
                  pl.BlockSpec((1, MEM_TOKENS, MEM_WIDTH), lambda b, t: (b, 0, k_col)),
                  pl.BlockSpec((1, MEM_TOKENS, MEM_WIDTH), lambda b, t: (b, 0, v_col))],
        out_specs=pl.BlockSpec((tb, MEM_WIDTH), lambda b, t: (b * nb + t, 0)),
        out_shape=jax.ShapeDtypeStruct((batch * rows_per_batch, MEM_WIDTH), BF16),
        compiler_params=_cparams(("arbitrary", "arbitrary"), 32),
        name=name,
    )(q, mk, mv)


def _router_kernel(h_ref, w_ref, b_ref, idx_ref, gate_ref):
    logits = jnp.dot(h_ref[...], w_ref[...], preferred_element_type=F32,
                     precision=lax.Precision.HIGHEST) + b_ref[...]
    lane = lax.broadcasted_iota(I32, logits.shape, 1)
    logits = jnp.where(lane < N_EXPERTS, logits, -jnp.inf)
    vals, idxs = [], []
    for _ in range(TOP_K):
        m = jnp.max(logits, axis=-1, keepdims=True)
        idx = jnp.min(jnp.where(logits == m, lane, LANES), axis=-1, keepdims=True)
        vals.append(m)
        idxs.append(idx)
        logits = jnp.where(lane == idx, -jnp.inf, logits)
    es = [jnp.exp(v - vals[0]) for v in vals]
    tot = es[0]
    for e in es[1:]:
        tot = tot + e
    idx_out = jnp.zeros(logits.shape, I32)
    gate_out = jnp.zeros(logits.shape, F32)
    for k in range(TOP_K):
        idx_out = jnp.where(lane == k, idxs[k], idx_out)
        gate_out = jnp.where(lane == k, es[k] / tot, gate_out)
    idx_ref[...] = idx_out
    gate_ref[...] = gate_out


def _router(h, w_router, b_router):
    rows, d = h.shape
    tm = NORM_ROWS
    wp = jnp.zeros((d, LANES), F32).at[:, :N_EXPERTS].set(w_router)
    bp = jnp.zeros((1, LANES), F32).at[0, :N_EXPERTS].set(b_router)
    return pl.pallas_call(
        _router_kernel,
        grid=(rows // tm,),
        in_specs=[pl.BlockSpec((tm, d), lambda i: (i, 0)),
                  pl.BlockSpec((d, LANES), lambda i: (0, 0)),
                  pl.BlockSpec((1, LANES), lambda i: (0, 0))],
        out_specs=[pl.BlockSpec((tm, LANES), lambda i: (i, 0)),
                   pl.BlockSpec((tm, LANES), lambda i: (i, 0))],
        out_shape=[jax.ShapeDtypeStruct((rows, LANES), I32),
                   jax.ShapeDtypeStruct((rows, LANES), F32)],
        compiler_params=_cparams(("arbitrary",), 40),
        name="router",
    )(h, wp, bp)


def _moe_tiles(n_assign):
    s_max = n_assign // MOE_NMAX + N_EXPERTS
    return s_max, s_max * MOE_NMAX


def _route(top_idx):
    n = top_idx.size
    s_max, rows_tot = _moe_tiles(n)
    e_flat = top_idx.reshape(-1)
    iota = jnp.arange(n, dtype=I32)
    _, order = lax.sort((e_flat, iota), num_keys=1, is_stable=True)
    _, rank_all = lax.sort((order, iota), num_keys=1)
    counts = jnp.sum((e_flat[:, None] == jnp.arange(N_EXPERTS, dtype=I32)[None, :]).astype(I32), axis=0)
    start = jnp.cumsum(counts) - counts
    n_st = (counts + MOE_NMAX - 1) // MOE_NMAX
    share = (counts + jnp.maximum(n_st, 1) - 1) // jnp.maximum(n_st, 1)
    q = jnp.maximum((share + MOE_R - 1) // MOE_R * MOE_R, MOE_R)
    st_end = jnp.cumsum(n_st)
    st_start = st_end - n_st
    n_live = st_end[-1]
    s_ids = jnp.arange(s_max, dtype=I32)
    st_e = jnp.minimum(jnp.searchsorted(st_end, s_ids, side="right"), N_EXPERTS - 1).astype(I32)
    t_in_e = s_ids - st_start[st_e]
    st_rows = jnp.clip(counts[st_e] - t_in_e * q[st_e], 0, q[st_e])
    st_rows = jnp.where(s_ids < n_live, st_rows, 0).astype(I32)
    st_src = (start[st_e] + t_in_e * q[st_e]).astype(I32)
    rank = rank_all - start[e_flat]
    pos = ((st_start[e_flat] + rank // q[e_flat]) * MOE_NMAX + rank % q[e_flat]).astype(I32)
    return dict(st_e=st_e, st_rows=st_rows, st_src=st_src, n_live=n_live.reshape(1).astype(I32),
                tok_sorted=(order // TOP_K).astype(I32), pos=pos)


def _gather_kernel(st_n, st_src, nblk_ref, tok_sorted, h_hbm, o_ref, buf, sem):
    i = pl.program_id(0)
    per = MOE_NMAX // GATHER_ROWS
    nblk = nblk_ref[0]
    slot = i % 2
    n_assign = tok_sorted.shape[0]

    def live(blk):
        return st_n[blk // per] > (blk % per) * GATHER_ROWS

    def row_copy(tok, k, s):
        return pltpu.make_async_copy(h_hbm.at[pl.ds(tok, 1), :], buf.at[s, pl.ds(k, 1), :], sem.at[s])

    def issue(blk, s):
        base = st_src[blk // per] + (blk % per) * GATHER_ROWS

        def body(k, c):
            row_copy(tok_sorted[jnp.minimum(base + k, n_assign - 1)], k, s).start()
            return c
        lax.fori_loop(0, GATHER_ROWS, body, 0, unroll=DMA_UNROLL)

    @pl.when((i == 0) & live(0))
    def _():
        issue(0, 0)

    nxt = jnp.minimum(i + 1, nblk - 1)

    @pl.when((i + 1 < nblk) & live(nxt))
    def _():
        issue(nxt, 1 - slot)

    @pl.when(live(i))
    def _():
        def wait(k, c):
            row_copy(0, k, slot).wait()
            return c
        lax.fori_loop(0, GATHER_ROWS, wait, 0, unroll=DMA_UNROLL)
        o_ref[...] = buf[slot].astype(o_ref.dtype)


def _gather(h, route, rows_tot):
    d = h.shape[1]
    per = MOE_NMAX // GATHER_ROWS
    nblk = route["n_live"] * per
    return pl.pallas_call(
        _gather_kernel,
        grid_spec=pltpu.PrefetchScalarGridSpec(
            num_scalar_prefetch=4,
            grid=(nblk[0],),
            in_specs=[pl.BlockSpec(memory_space=pl.ANY)],
            out_specs=pl.BlockSpec((GATHER_ROWS, d), lambda i, *_: (i, 0)),
            scratch_shapes=[pltpu.VMEM((2, GATHER_ROWS, d), F32), pltpu.SemaphoreType.DMA((2,))]),
        out_shape=jax.ShapeDtypeStruct((rows_tot, d), BF16),
        compiler_params=_cparams(("arbitrary",), 32),
        name="moe_gather",
    )(route["st_rows"], route["st_src"], nblk, route["tok_sorted"], h)


def _row_passes(cnt):
    return (cnt + MOE_R - 1) // MOE_R


def _for_row_count(cnt, fn):
    npass = _row_passes(cnt)
    for k in range(1, MOE_NMAX // MOE_R + 1):
        pl.when(npass == k)(functools.partial(fn, k * MOE_R))


def _moe_up_kernel(st_e, st_n, xs_ref, wg_ref, wu_ref, bg_ref, bu_ref, o_ref):
    def compute(rows):
        x = xs_ref[0:rows, :]
        g = jnp.dot(x, wg_ref[0].astype(BF16), preferred_element_type=F32) + bg_ref[0]
        u = jnp.dot(x, wu_ref[0].astype(BF16), preferred_element_type=F32) + bu_ref[0]
        g = jnp.minimum(g, SWIGLU_LIMIT)
        u = jnp.clip(u, -SWIGLU_LIMIT, SWIGLU_LIMIT)
        o_ref[0:rows, :] = ((u + 1.0) * (g * jax.nn.sigmoid(SWIGLU_ALPHA * g))).astype(o_ref.dtype)

    _for_row_count(st_n[pl.program_id(0)], compute)


def _moe_down_kernel(st_e, st_n, a_ref, wd_ref, bd_ref, o_ref):
    def compute(rows):
        o_ref[0:rows, :] = jnp.dot(a_ref[0:rows, :], wd_ref[0].astype(BF16),
                                   preferred_element_type=F32) + bd_ref[0]

    _for_row_count(st_n[pl.program_id(0)], compute)


def _w_cols(off):
    return lambda s, j, st_e, st_n: (st_e[s], 0, off + j)


def _moe_up(xs, route, w_gate_up, b_gate_up):
    rows_tot, d = xs.shape
    nj = D_EXPERT // MOE_TN_UP
    bias = b_gate_up.reshape(N_EXPERTS, 1, 2 * D_EXPERT)
    return pl.pallas_call(
        _moe_up_kernel,
        grid_spec=pltpu.PrefetchScalarGridSpec(
            num_scalar_prefetch=2,
            grid=(route["n_live"][0], nj),
            in_specs=[pl.BlockSpec((MOE_NMAX, d), lambda s, j, st_e, st_n: (s, 0)),
                      pl.BlockSpec((1, d, MOE_TN_UP), _w_cols(0)),
                      pl.BlockSpec((1, d, MOE_TN_UP), _w_cols(nj)),
                      pl.BlockSpec((1, 1, MOE_TN_UP), _w_cols(0)),
                      pl.BlockSpec((1, 1, MOE_TN_UP), _w_cols(nj))],
            out_specs=pl.BlockSpec((MOE_NMAX, MOE_TN_UP), lambda s, j, st_e, st_n: (s, j))),
        out_shape=jax.ShapeDtypeStruct((rows_tot, D_EXPERT), BF16),
        compiler_params=_cparams(("arbitrary", "arbitrary"), 52),
        name="moe_up",
    )(route["st_e"], route["st_rows"], xs, w_gate_up, w_gate_up, bias, bias)


def _moe_down(act, route, w_down, b_down):
    rows_tot, f = act.shape
    d = w_down.shape[2]
    nj = d // MOE_TN_DOWN
    return pl.pallas_call(
        _moe_down_kernel,
        grid_spec=pltpu.PrefetchScalarGridSpec(
            num_scalar_prefetch=2,
            grid=(route["n_live"][0], nj),
            in_specs=[pl.BlockSpec((MOE_NMAX, f), lambda s, j, st_e, st_n: (s, 0)),
                      pl.BlockSpec((1, f, MOE_TN_DOWN), _w_cols(0)),
                      pl.BlockSpec((1, 1, MOE_TN_DOWN), _w_cols(0))],
            out_specs=pl.BlockSpec((MOE_NMAX, MOE_TN_DOWN), lambda s, j, st_e, st_n: (s, j))),
        out_shape=jax.ShapeDtypeStruct((rows_tot, d), F32),
        compiler_params=_cparams(("arbitrary", "arbitrary"), 52),
        name="moe_down",
    )(route["st_e"], route["st_rows"], act, w_down, b_down.reshape(N_EXPERTS, 1, d))


def _combine_kernel(pos_ref, posn_ref, ys_hbm, x_ref, gate_ref, g_ref, op_ref, os_ref, buf, sem,
                    *, n_blk, n_prompt_blk):
    i = pl.program_id(0)
    slot = i % 2
    n = COMBINE_TOK * TOP_K

    def row_copy(p, t, k, s):
        return pltpu.make_async_copy(ys_hbm.at[pl.ds(p, 1), :], buf.at[s, k, pl.ds(t, 1), :], sem.at[s])

    def issue(pos_smem, s):
        def body(t, c):
            for k in range(TOP_K):
                row_copy(pos_smem[t * TOP_K + k], t, k, s).start()
            return c
        lax.fori_loop(0, COMBINE_TOK, body, 0, unroll=DMA_UNROLL // TOP_K)

    @pl.when(i == 0)
    def _():
        issue(pos_ref, 0)

    @pl.when(i + 1 < n_blk)
    def _():
        issue(posn_ref, 1 - slot)

    def wait(t, c):
        for k in range(TOP_K):
            row_copy(0, t, k, slot).wait()
        return c

    lax.fori_loop(0, COMBINE_TOK, wait, 0, unroll=DMA_UNROLL // TOP_K)
    x = x_ref[...]
    for k in range(TOP_K):
        x = x + buf[slot, k] * gate_ref[:, k:k + 1]
    ms = jnp.mean(x * x, axis=-1, keepdims=True)
    y = x * lax.rsqrt(ms + NORM_EPS) * g_ref[...]

    @pl.when(i < n_prompt_blk)
    def _():
        op_ref[...] = y

    @pl.when(i >= n_prompt_blk)
    def _():
        os_ref[...] = y


def _combine(ys, pos, x, gates, g_final, rows_prompt):
    rows, d = x.shape
    n_blk = rows // COMBINE_TOK
    npb = rows_prompt // COMBINE_TOK
    nidx = COMBINE_TOK * TOP_K
    return pl.pallas_call(
        functools.partial(_combine_kernel, n_blk=n_blk, n_prompt_blk=npb),
        grid=(n_blk,),
        in_specs=[pl.BlockSpec((nidx,), lambda i: (i,), memory_space=pltpu.SMEM),
                  pl.BlockSpec((nidx,), lambda i: (jnp.minimum(i + 1, n_blk - 1),), memory_space=pltpu.SMEM),
                  pl.BlockSpec(memory_space=pl.ANY),
                  pl.BlockSpec((COMBINE_TOK, d), lambda i: (i, 0)),
                  pl.BlockSpec((COMBINE_TOK, LANES), lambda i: (i, 0)),
                  pl.BlockSpec((1, d), lambda i: (0, 0))],
        out_specs=[pl.BlockSpec((COMBINE_TOK, d), lambda i: (jnp.minimum(i, npb - 1), 0)),
                   pl.BlockSpec((COMBINE_TOK, d), lambda i: (jnp.maximum(i - npb, 0), 0))],
        out_shape=[jax.ShapeDtypeStruct((rows_prompt, d), F32),
                   jax.ShapeDtypeStruct((rows - rows_prompt, d), F32)],
        scratch_shapes=[pltpu.VMEM((2, TOP_K, COMBINE_TOK, d), F32), pltpu.SemaphoreType.DMA((2,))],
        compiler_params=_cparams(("arbitrary",), 32),
        name="moe_combine",
    )(pos, pos, ys, x, gates, g_final.reshape(1, d))


def _tail_rows(z, batch, seq, nrows, c0, c1):
    return jnp.stack([lax.slice(z, ((b + 1) * seq - nrows, c0), ((b + 1) * seq, c1)) for b in range(batch)])


def kernel(x_prompt, x_sample, cache_band_k, cache_band_v, state_conv, cache_mem_k, cache_mem_v, mem_prompt, norm_mix, w_in, rel_table, conv_w, conv_b, conv_ln_g, conv_ln_b, grp_g_att, grp_g_conv, w_out, norm_mem_q, norm_mem_kv, w_mem_q, w_mem_k, w_mem_v, w_mem_o, norm_ffn, w_router, b_router, w_gate_up, b_gate_up, w_down, b_down, norm_final):
    depth = norm_mix.shape[0]
    assert depth == 1
    bp, sp, d = x_prompt.shape
    bs, ss, _ = x_sample.shape
    tp, ts = bp * sp, bs * ss
    t_all = tp + ts
    assert d == D_MODEL and sp % QB == 0 and sp % CONV_TB == 0 and tp % MM_ROWS == 0
    assert sp >= BAND_WINDOW and ss == CHUNK and t_all % MM_ROWS == 0
    assert tp % COMBINE_TOK == 0 and ts % COMBINE_TOK == 0
    l = 0

    x = (x_prompt.reshape(tp, d), x_sample.reshape(ts, d))

    h = _rmsnorm(x, norm_mix[l], BF16)
    z = _matmul([h], w_in[l].astype(BF16), None, F32, 1024, "in_proj")
    w_rows = _rel_rows(rel_table[l])
    att_p = _attn_prompt(z, w_rows, grp_g_att[l], bp, sp)
    ck = cache_band_k[l].reshape(bs, -1, HEAD_DIM)
    cv = cache_band_v[l].reshape(bs, -1, HEAD_DIM)
    att_s = _attn_sample(z, tp, ck, cv, w_rows, grp_g_att[l], bs, ss)
    conv_params = (conv_w[l], conv_b[l].reshape(1, -1), conv_ln_g[l].reshape(1, -1),
                   conv_ln_b[l].reshape(1, -1), grp_g_conv[l].reshape(1, -1))
    cnv_p, cs_p = _conv_prompt(z, conv_params, bp, sp)
    past = jnp.pad(state_conv[l], ((0, 0), (CONV_HALO - (CONV_K - 1), 0), (0, 0)))
    cnv_s, cs_s = _conv_sample(z, tp, past, conv_params, bs, ss)
    x1 = _matmul([(att_p, att_s), (cnv_p, cnv_s)], w_out[l].astype(BF16), x, F32, 256, "out_proj")

    hd = (ATT_HEADS, HEAD_DIM)
    bk_p = _tail_rows(z, bp, sp, BAND_WINDOW, ATT_WIDTH, 2 * ATT_WIDTH).reshape(1, bp, BAND_WINDOW, *hd)
    bv_p = _tail_rows(z, bp, sp, BAND_WINDOW, 2 * ATT_WIDTH, 3 * ATT_WIDTH).reshape(1, bp, BAND_WINDOW, *hd)
    bk_s = lax.slice(z, (tp, ATT_WIDTH), (t_all, 2 * ATT_WIDTH)).reshape(1, bs, ss, *hd)
    bv_s = lax.slice(z, (tp, 2 * ATT_WIDTH), (t_all, 3 * ATT_WIDTH)).reshape(1, bs, ss, *hd)
    keep = CONV_HALO - (CONV_K - 1)
    cs_p = cs_p[:, keep:][None]
    cs_s = cs_s[:, keep:][None]

    nm = mem_prompt.shape[1]
    mem_n = _rmsnorm(mem_prompt.reshape(bp * nm, d), norm_mem_kv[l], BF16)
    w_mkv = jnp.concatenate([w_mem_k[l], w_mem_v[l]], axis=1).astype(BF16)
    mkv = _matmul([mem_n], w_mkv, None, F32, 2 * MEM_WIDTH, "mem_kv").reshape(bp, nm, 2 * MEM_WIDTH)
    mk_p = mkv[:, :, :MEM_WIDTH].reshape(1, bp, nm, MEM_HEADS, HEAD_DIM)
    mv_p = mkv[:, :, MEM_WIDTH:].reshape(1, bp, nm, MEM_HEADS, HEAD_DIM)
    hq = _rmsnorm(x1, norm_mem_q[l], BF16)
    q = _matmul([hq], w_mem_q[l].astype(BF16), None, F32, MEM_WIDTH, "mem_q")
    mo_p = _memattn(q, 0, mkv, mkv, 0, 1, bp, sp, 512, "memattn_prompt")
    cmk = cache_mem_k[l].reshape(bs, nm, MEM_WIDTH)
    cmv = cache_mem_v[l].reshape(bs, nm, MEM_WIDTH)
    mo_s = _memattn(q, tp, cmk, cmv, 0, 0, bs, ss, ss, "memattn_sample")
    x2 = _matmul([(mo_p, mo_s)], w_mem_o[l].astype(BF16), x1, F32, 512, "mem_o")

    hf = _rmsnorm(x2, norm_ffn[l], F32)
    top_idx, gates = _router(hf, w_router[l], b_router[l])
    route = _route(top_idx[:, :TOP_K])
    _, rows_tot = _moe_tiles(t_all * TOP_K)
    xs = _gather(hf, route, rows_tot)
    act = _moe_up(xs, route, w_gate_up[l], b_gate_up[l])
    ys = _moe_down(act, route, w_down[l], b_down[l])
    y_p, y_s = _combine(ys, route["pos"], x2, gates, norm_final, tp)

    return (y_p.reshape(bp, sp, d), y_s.reshape(bs, ss, d), bk_p, bv_p, cs_p, mk_p, mv_p, bk_s, bv_s, cs_s)
```

```python
import functools

import jax
import jax.numpy as jnp
from jax import lax
from jax.experimental import pallas as pl
from jax.experimental.pallas import tpu as pltpu

F32 = jnp.float32
BF16 = jnp.bfloat16
I32 = jnp.int32

D_MODEL = 4096
CHUNK = 64
LEFT_CHUNKS = 8
BAND_WINDOW = LEFT_CHUNKS * CHUNK
HEAD_DIM = 128
ATT_WIDTH = 2048
CONV_WIDTH = 2048
ATT_HEADS = 16
IN_WIDTH = 3 * ATT_WIDTH + 2 * CONV_WIDTH
MAX_REL = 256
CONV_K = 31
MEM_TOKENS = 256
MEM_HEADS = 4
MEM_WIDTH = 512
N_EXPERTS = 32
TOP_K = 4
D_EXPERT = 4096
SWIGLU_LIMIT = 7.0
SWIGLU_ALPHA = 1.702
NORM_EPS = 1e-5
NEG_INF = -1e30

LANES = 128
SUBLANES = 8

NORM_ROWS = 512
MM_ROWS = 1024
ATTN_HG = 8
QB = 256
KWIN = QB + BAND_WINDOW
ROLL_W = 1024
SKEYS = 640
CONV_TB = 256
CONV_HALO = 32
CONV_CW = 512
MOE_R = 128
MOE_NMAX = 1536
MOE_TN_UP = 256
MOE_TN_DOWN = 512
GATHER_ROWS = 256
COMBINE_TOK = 64
DMA_UNROLL = 8


def _cparams(sem, vmem_mib):
    return pltpu.CompilerParams(dimension_semantics=sem, vmem_limit_bytes=vmem_mib * 2**20)


def _pair_specs(pair, tm, width, col_map=None):
    first, second = pair
    assert first.shape[0] % tm == 0 and second.shape[0] % tm == 0
    nf = first.shape[0] // tm
    col = col_map if col_map is not None else (lambda *ij: 0)
    specs = [pl.BlockSpec((tm, width), lambda *ij: (jnp.minimum(ij[0], nf - 1), col(*ij))),
             pl.BlockSpec((tm, width), lambda *ij: (jnp.maximum(ij[0] - nf, 0), col(*ij)))]
    return specs, nf


def _pair_value(refs, nf):
    return jnp.where(pl.program_id(0) < nf, refs[0][...], refs[1][...])


def _rmsnorm_kernel(*refs, nf):
    g_ref, o_ref = refs[-2], refs[-1]
    x = refs[0][...] if nf is None else _pair_value(refs[:2], nf)
    ms = jnp.mean(x * x, axis=-1, keepdims=True)
    o_ref[...] = (x * lax.rsqrt(ms + NORM_EPS) * g_ref[...]).astype(o_ref.dtype)


def _rmsnorm(x, g, out_dtype):
    if isinstance(x, tuple):
        rows, d = x[0].shape[0] + x[1].shape[0], x[0].shape[1]
        tm = NORM_ROWS // 2
        x_specs, nf = _pair_specs(x, tm, d)
        xs = list(x)
    else:
        rows, d = x.shape
        tm = min(NORM_ROWS, rows)
        x_specs, nf, xs = [pl.BlockSpec((tm, d), lambda i: (i, 0))], None, [x]
    return pl.pallas_call(
        functools.partial(_rmsnorm_kernel, nf=nf),
        grid=(rows // tm,),
        in_specs=x_specs + [pl.BlockSpec((1, d), lambda i: (0, 0))],
        out_specs=pl.BlockSpec((tm, d), lambda i: (i, 0)),
        out_shape=jax.ShapeDtypeStruct((rows, d), out_dtype),
        compiler_params=_cparams(("arbitrary",), 48),
        name="rmsnorm",
    )(*xs, g.reshape(1, d))


def _mm_kernel(*refs, layout, nf):
    a_counts, res_count = layout[:-1], layout[-1]
    pos = 0
    a_vals = []
    for c in a_counts:
        a_vals.append(refs[pos][...] if c == 1 else _pair_value(refs[pos:pos + 2], nf))
        pos += c
    b_refs = refs[pos:pos + len(a_counts)]
    pos += len(a_counts)
    acc = jnp.dot(a_vals[0], b_refs[0][...], preferred_element_type=F32)
    for k in range(1, len(a_vals)):
        acc = acc + jnp.dot(a_vals[k], b_refs[k][...], preferred_element_type=F32)
    if res_count:
        acc = acc + (refs[pos][...] if res_count == 1 else _pair_value(refs[pos:pos + 2], nf))
    refs[-1][...] = acc.astype(refs[-1].dtype)


def _rows_of(x):
    return x[0].shape[0] + x[1].shape[0] if isinstance(x, tuple) else x.shape[0]


def _matmul(a_parts, w, res, out_dtype, tn, name, tm=MM_ROWS):
    m = _rows_of(a_parts[0])
    kk = (a_parts[0][0] if isinstance(a_parts[0], tuple) else a_parts[0]).shape[1]
    n = w.shape[1]
    tm = min(tm, m)
    in_specs, args, layout, nf = [], [], [], None
    for a in a_parts:
        if isinstance(a, tuple):
            specs, nf = _pair_specs(a, tm, kk)
            in_specs += specs
            args += list(a)
            layout.append(2)
        else:
            in_specs.append(pl.BlockSpec((tm, kk), lambda i, j: (i, 0)))
            args.append(a)
            layout.append(1)
    in_specs += [pl.BlockSpec((kk, tn), functools.partial(lambda i, j, p: (p, j), p=p))
                 for p in range(len(a_parts))]
    args += [w] * len(a_parts)
    if isinstance(res, tuple):
        specs, nf = _pair_specs(res, tm, tn, col_map=lambda i, j: j)
        in_specs += specs
        args += list(res)
        layout.append(2)
    elif res is not None:
        in_specs.append(pl.BlockSpec((tm, tn), lambda i, j: (i, j)))
        args.append(res)
        layout.append(1)
    else:
        layout.append(0)
    return pl.pallas_call(
        functools.partial(_mm_kernel, layout=tuple(layout), nf=nf),
        grid=(m // tm, n // tn),
        in_specs=in_specs,
        out_specs=pl.BlockSpec((tm, tn), lambda i, j: (i, j)),
        out_shape=jax.ShapeDtypeStruct((m, n), out_dtype),
        compiler_params=_cparams(("arbitrary", "arbitrary"), 56),
        name=name,
    )(*args)


def _rel_rows(table):
    m = jnp.arange(ROLL_W)
    m = jnp.where(m < KWIN, m, m - ROLL_W)
    idx = jnp.clip(BAND_WINDOW - m, -MAX_REL, MAX_REL) + MAX_REL
    return table[:, idx].astype(F32)


def _toeplitz_bias(w_row, rows):
    wb = jnp.broadcast_to(w_row, (rows, ROLL_W))
    return pltpu.roll(wb, 0, 1, stride=1, stride_axis=0)


def _group_rmsnorm_store(o, g_ref, o_ref):
    ms = jnp.mean(o * o, axis=-1, keepdims=True)
    o_ref[...] = (o * lax.rsqrt(ms + NORM_EPS) * g_ref[...]).astype(o_ref.dtype)


def _attn_prompt_kernel(w_ref, q_ref, k0_ref, k1_ref, k2_ref, v0_ref, v1_ref, v2_ref, g_ref,
                        o_ref, bias_sc, o_sc):
    i = pl.program_id(1)
    hg = pl.program_id(2)
    first = (pl.program_id(0) == 0) & (i == 0) & (hg == 0)

    @pl.when(first)
    def _():
        r = lax.broadcasted_iota(I32, (QB, KWIN), 0) // CHUNK
        c = lax.broadcasted_iota(I32, (QB, KWIN), 1) // CHUNK
        band = (c >= r) & (c <= r + LEFT_CHUNKS)
        for h in range(ATT_HEADS):
            t = _toeplitz_bias(w_ref[h:h + 1, :], QB)
            bias_sc[h] = jnp.where(band, t[:, :KWIN], NEG_INF)

    kk = jnp.concatenate([r[...].astype(BF16) for r in (k0_ref, k1_ref, k2_ref)], axis=0)
    vv = jnp.concatenate([r[...].astype(BF16) for r in (v0_ref, v1_ref, v2_ref)], axis=0)
    col = lax.broadcasted_iota(I32, (QB, KWIN), 1)
    in_seq = col >= (BAND_WINDOW // QB - i) * QB
    scale = HEAD_DIM ** -0.5
    for hh in range(ATTN_HG):
        hs = slice(hh * HEAD_DIM, (hh + 1) * HEAD_DIM)
        qh = q_ref[:, hs].astype(BF16)
        s = lax.dot_general(qh, kk[:, hs], (((1,), (1,)), ((), ())),
                            preferred_element_type=F32) * scale + bias_sc[hg * ATTN_HG + hh]
        s = jnp.where(in_seq, s, NEG_INF)
        m = jnp.max(s, axis=-1, keepdims=True)
        p = jnp.exp(s - m)
        l = jnp.sum(p, axis=-1, keepdims=True)
        o = jnp.dot(p.astype(BF16), vv[:, hs], preferred_element_type=F32)
        o_sc[hg, :, hs] = o / l

    @pl.when(hg == ATT_HEADS // ATTN_HG - 1)
    def _():
        o = jnp.concatenate([o_sc[g] for g in range(ATT_HEADS // ATTN_HG)], axis=1)
        _group_rmsnorm_store(o, g_ref, o_ref)


def _attn_prompt(z, w_rows, g_att, batch, seq):
    nqb = seq // QB
    back = BAND_WINDOW // QB
    gw = ATTN_HG * HEAD_DIM
    ngrp = ATT_HEADS // ATTN_HG

    def kv_spec(col0, d):
        return pl.BlockSpec((QB, gw),
                            lambda b, i, g: (b * nqb + jnp.maximum(i - d, 0), col0 + g))

    kc, vc = ATT_WIDTH // gw, 2 * ATT_WIDTH // gw
    return pl.pallas_call(
        _attn_prompt_kernel,
        grid=(batch, nqb, ngrp),
        in_specs=[pl.BlockSpec((ATT_HEADS, ROLL_W), lambda b, i, g: (0, 0)),
                  pl.BlockSpec((QB, gw), lambda b, i, g: (b * nqb + i, g)),
                  kv_spec(kc, back), kv_spec(kc, back - 1), kv_spec(kc, 0),
                  kv_spec(vc, back), kv_spec(vc, back - 1), kv_spec(vc, 0),
                  pl.BlockSpec((1, ATT_WIDTH), lambda b, i, g: (0, 0))],
        out_specs=pl.BlockSpec((QB, ATT_WIDTH), lambda b, i, g: (b * nqb + i, 0)),
        out_shape=jax.ShapeDtypeStruct((batch * seq, ATT_WIDTH), BF16),
        scratch_shapes=[pltpu.VMEM((ATT_HEADS, QB, KWIN), F32),
                        pltpu.VMEM((ngrp, QB, gw), F32)],
        compiler_params=_cparams(("arbitrary", "arbitrary", "arbitrary"), 48),
        name="attn_prompt",
    )(w_rows, z, z, z, z, z, z, z, g_att.reshape(1, ATT_WIDTH))


def _attn_sample_kernel(w_ref, q_ref, kn_ref, vn_ref, kc_ref, vc_ref, g_ref, o_ref, bias_sc, o_sc,
                        *, past, new):
    @pl.when(pl.program_id(0) == 0)
    def _():
        c = lax.broadcasted_iota(I32, (new, SKEYS), 1)
        for h in range(ATT_HEADS):
            t = _toeplitz_bias(w_ref[h:h + 1, :], new)
            bias_sc[h] = jnp.where(c < past + new, t[:, :SKEYS], NEG_INF)

    pad = jnp.zeros((SKEYS - past - new, HEAD_DIM), BF16)
    scale = HEAD_DIM ** -0.5
    for h in range(ATT_HEADS):
        hs = slice(h * HEAD_DIM, (h + 1) * HEAD_DIM)
        head_rows = pl.ds(h, past, stride=ATT_HEADS)
        kh = jnp.concatenate([kc_ref[0, head_rows, :].astype(BF16), kn_ref[:, hs].astype(BF16), pad], axis=0)
        vh = jnp.concatenate([vc_ref[0, head_rows, :].astype(BF16), vn_ref[:, hs].astype(BF16), pad], axis=0)
        qh = q_ref[:, hs].astype(BF16)
        s = lax.dot_general(qh, kh, (((1,), (1,)), ((), ())),
                            preferred_element_type=F32) * scale + bias_sc[h]
        m = jnp.max(s, axis=-1, keepdims=True)
        p = jnp.exp(s - m)
        l = jnp.sum(p, axis=-1, keepdims=True)
        o = jnp.dot(p.astype(BF16), vh, preferred_element_type=F32)
        o_sc[:, hs] = o / l
    _group_rmsnorm_store(o_sc[...], g_ref, o_ref)


def _attn_sample(z, row0, cache_k, cache_v, w_rows, g_att, batch, new):
    past = cache_k.shape[1] // ATT_HEADS
    assert past == BAND_WINDOW and past + new <= SKEYS and row0 % new == 0
    blk0 = row0 // new
    return pl.pallas_call(
        functools.partial(_attn_sample_kernel, past=past, new=new),
        grid=(batch,),
        in_specs=[pl.BlockSpec((ATT_HEADS, ROLL_W), lambda b: (0, 0)),
                  pl.BlockSpec((new, ATT_WIDTH), lambda b: (blk0 + b, 0)),
                  pl.BlockSpec((new, ATT_WIDTH), lambda b: (blk0 + b, 1)),
                  pl.BlockSpec((new, ATT_WIDTH), lambda b: (blk0 + b, 2)),
                  pl.BlockSpec((1, past * ATT_HEADS, HEAD_DIM), lambda b: (b, 0, 0)),
                  pl.BlockSpec((1, past * ATT_HEADS, HEAD_DIM), lambda b: (b, 0, 0)),
                  pl.BlockSpec((1, ATT_WIDTH), lambda b: (0, 0))],
        out_specs=pl.BlockSpec((new, ATT_WIDTH), lambda b: (b, 0)),
        out_shape=jax.ShapeDtypeStruct((batch * new, ATT_WIDTH), BF16),
        scratch_shapes=[pltpu.VMEM((ATT_HEADS, new, SKEYS), F32),
                        pltpu.VMEM((new, ATT_WIDTH), F32)],
        compiler_params=_cparams(("arbitrary",), 48),
        name="attn_sample",
    )(w_rows, z, z, z, cache_k, cache_v, g_att.reshape(1, ATT_WIDTH))


def _conv_tail(u, hist, w_ref, cb_ref, lg_ref, lb_ref, gg_ref, o_ref, st_ref, u_sc, y_sc, s_sc, tb):
    u_sc[0:CONV_HALO, :] = hist
    u_sc[CONV_HALO:CONV_HALO + tb, :] = u
    u_sc[CONV_HALO + tb:CONV_HALO + tb + SUBLANES, :] = jnp.zeros((SUBLANES, CONV_WIDTH), F32)
    st_ref[0] = u[tb - CONV_HALO:, :]
    off = CONV_HALO - (CONV_K - 1)
    srows = tb + SUBLANES
    for cc in range(CONV_WIDTH // CONV_CW):
        cs = slice(cc * CONV_CW, (cc + 1) * CONV_CW)
        for p in range(SUBLANES):
            acc = None
            for j in range(CONV_K):
                if (off + j) % SUBLANES != p:
                    continue
                a = off + j - p
                term = u_sc[a:a + srows, cs] * w_ref[j:j + 1, cs]
                acc = term if acc is None else acc + term
            s_sc[p] = acc
        y = s_sc[0, 0:tb, :]
        for p in range(1, SUBLANES):
            y = y + s_sc[p, p:p + tb, :]
        y_sc[:, cs] = y + cb_ref[:, cs]
    y = y_sc[...]
    mu = jnp.mean(y, axis=-1, keepdims=True)
    yc = y - mu
    y = yc * lax.rsqrt(jnp.mean(yc * yc, axis=-1, keepdims=True) + NORM_EPS)
    y = y * lg_ref[...] + lb_ref[...]
    y = y * jax.nn.sigmoid(y)
    _group_rmsnorm_store(y, gg_ref, o_ref)


def _conv_prompt_kernel(cv_ref, cg_ref, pv_ref, pg_ref, w_ref, cb_ref, lg_ref, lb_ref, gg_ref,
                        o_ref, st_ref, u_sc, y_sc, s_sc):
    u = cv_ref[...] * jax.nn.sigmoid(cg_ref[...])
    hist = pv_ref[...] * jax.nn.sigmoid(pg_ref[...])
    hist = jnp.where(pl.program_id(1) > 0, hist, 0.0)
    _conv_tail(u, hist, w_ref, cb_ref, lg_ref, lb_ref, gg_ref, o_ref, st_ref, u_sc, y_sc, s_sc, CONV_TB)


def _conv_sample_kernel(cv_ref, cg_ref, past_ref, w_ref, cb_ref, lg_ref, lb_ref, gg_ref,
                        o_ref, st_ref, u_sc, y_sc, s_sc, *, tb):
    u = cv_ref[...] * jax.nn.sigmoid(cg_ref[...])
    _conv_tail(u, past_ref[0], w_ref, cb_ref, lg_ref, lb_ref, gg_ref, o_ref, st_ref, u_sc, y_sc, s_sc, tb)


def _conv_scratch(tb):
    return [pltpu.VMEM((CONV_HALO + tb + SUBLANES, CONV_WIDTH), F32),
            pltpu.VMEM((tb, CONV_WIDTH), F32),
            pltpu.VMEM((SUBLANES, tb + SUBLANES, CONV_CW), F32)]


def _conv_param_specs(nidx):
    zero = (lambda b, t: (0, 0)) if nidx == 2 else (lambda b: (0, 0))
    return [pl.BlockSpec((CONV_K, CONV_WIDTH), zero)] + [pl.BlockSpec((1, CONV_WIDTH), zero)] * 4


def _conv_prompt(z, params, batch, seq):
    ntb = seq // CONV_TB
    per = CONV_TB // CONV_HALO
    cv_blk, cg_blk = 3 * ATT_WIDTH // CONV_WIDTH, 3 * ATT_WIDTH // CONV_WIDTH + 1

    def cur(col):
        return pl.BlockSpec((CONV_TB, CONV_WIDTH), lambda b, t: (b * ntb + t, col))

    def prev(col):
        return pl.BlockSpec((CONV_HALO, CONV_WIDTH),
                            lambda b, t: (jnp.maximum((b * ntb + t) * per - 1, 0), col))

    return pl.pallas_call(
        _conv_prompt_kernel,
        grid=(batch, ntb),
        in_specs=[cur(cv_blk), cur(cg_blk), prev(cv_blk), prev(cg_blk)] + _conv_param_specs(2),
        out_specs=[pl.BlockSpec((CONV_TB, CONV_WIDTH), lambda b, t: (b * ntb + t, 0)),
                   pl.BlockSpec((1, CONV_HALO, CONV_WIDTH), lambda b, t: (b, 0, 0))],
        out_shape=[jax.ShapeDtypeStruct((batch * seq, CONV_WIDTH), BF16),
                   jax.ShapeDtypeStruct((batch, CONV_HALO, CONV_WIDTH), F32)],
        scratch_shapes=_conv_scratch(CONV_TB),
        compiler_params=_cparams(("arbitrary", "arbitrary"), 48),
        name="conv_prompt",
    )(z, z, z, z, *params)


def _conv_sample(z, row0, past, params, batch, new):
    assert row0 % new == 0 and new >= CONV_HALO
    blk0 = row0 // new
    cv_blk, cg_blk = 3 * ATT_WIDTH // CONV_WIDTH, 3 * ATT_WIDTH // CONV_WIDTH + 1
    return pl.pallas_call(
        functools.partial(_conv_sample_kernel, tb=new),
        grid=(batch,),
        in_specs=[pl.BlockSpec((new, CONV_WIDTH), lambda b: (blk0 + b, cv_blk)),
                  pl.BlockSpec((new, CONV_WIDTH), lambda b: (blk0 + b, cg_blk)),
                  pl.BlockSpec((1, CONV_HALO, CONV_WIDTH), lambda b: (b, 0, 0))] + _conv_param_specs(1),
        out_specs=[pl.BlockSpec((new, CONV_WIDTH), lambda b: (b, 0)),
                   pl.BlockSpec((1, CONV_HALO, CONV_WIDTH), lambda b: (b, 0, 0))],
        out_shape=[jax.ShapeDtypeStruct((batch * new, CONV_WIDTH), BF16),
                   jax.ShapeDtypeStruct((batch, CONV_HALO, CONV_WIDTH), F32)],
        scratch_shapes=_conv_scratch(new),
        compiler_params=_cparams(("arbitrary",), 32),
        name="conv_sample",
    )(z, z, past, *params)


def _memattn_kernel(q_ref, mk_ref, mv_ref, o_ref):
    mk = mk_ref[0].astype(BF16)
    mv = mv_ref[0].astype(BF16)
    scale = HEAD_DIM ** -0.5
    for h in range(MEM_HEADS):
        hs = slice(h * HEAD_DIM, (h + 1) * HEAD_DIM)
        qh = q_ref[:, hs].astype(BF16)
        s = lax.dot_general(qh, mk[:, hs], (((1,), (1,)), ((), ())),
                            preferred_element_type=F32) * scale
        m = jnp.max(s, axis=-1, keepdims=True)
        p = jnp.exp(s - m)
        l = jnp.sum(p, axis=-1, keepdims=True)
        o = jnp.dot(p.astype(BF16), mv[:, hs], preferred_element_type=F32)
        o_ref[:, hs] = (o / l).astype(o_ref.dtype)


def _memattn(q, row0, mk, mv, k_col, v_col, batch, rows_per_batch, tb, name):
    nb = rows_per_batch // tb
    blk0 = row0 // tb
    return pl.pallas_call(
        _memattn_kernel,
        grid=(batch, nb),
        in_specs=[pl.BlockSpec((tb, MEM_WIDTH), lambda b, t: (blk0 + b * nb + t, 0)),
                  pl.BlockSpec((1, MEM_TOKENS, MEM_WIDTH), lambda b, t: (b, 0, k_col)),
                  pl.BlockSpec((1, MEM_TOKENS, MEM_WIDTH), lambda b, t: (b, 0, v_col))],
        out_specs=pl.BlockSpec((tb, MEM_WIDTH), lambda b, t: (b * nb + t, 0)),
        out_shape=jax.ShapeDtypeStruct((batch * rows_per_batch, MEM_WIDTH), BF16),
        compiler_params=_cparams(("arbitrary", "arbitrary"), 32),
        name=name,
    )(q, mk, mv)


def _router_kernel(h_ref, w_ref, b_ref, idx_ref, gate_ref):
    logits = jnp.dot(h_ref[...], w_ref[...], preferred_element_type=F32,
                     precision=lax.Precision.HIGHEST) + b_ref[...]
    lane = lax.broadcasted_iota(I32, logits.shape, 1)
    logits = jnp.where(lane < N_EXPERTS, logits, -jnp.inf)
    vals, idxs = [], []
    for _ in range(TOP_K):
        m = jnp.max(logits, axis=-1, keepdims=True)
        idx = jnp.min(jnp.where(logits == m, lane, LANES), axis=-1, keepdims=True)
        vals.append(m)
        idxs.append(idx)
        logits = jnp.where(lane == idx, -jnp.inf, logits)
    es = [jnp.exp(v - vals[0]) for v in vals]
    tot = es[0]
    for e in es[1:]:
        tot = tot + e
    idx_out = jnp.zeros(logits.shape, I32)
    gate_out = jnp.zeros(logits.shape, F32)
    for k in range(TOP_K):
        idx_out = jnp.where(lane == k, idxs[k], idx_out)
        gate_out = jnp.where(lane == k, es[k] / tot, gate_out)
    idx_ref[...] = idx_out
    gate_ref[...] = gate_out


def _router(h, w_router, b_router):
    rows, d = h.shape
    tm = NORM_ROWS
    wp = jnp.zeros((d, LANES), F32).at[:, :N_EXPERTS].set(w_router)
    bp = jnp.zeros((1, LANES), F32).at[0, :N_EXPERTS].set(b_router)
    return pl.pallas_call(
        _router_kernel,
        grid=(rows // tm,),
        in_specs=[pl.BlockSpec((tm, d), lambda i: (i, 0)),
                  pl.BlockSpec((d, LANES), lambda i: (0, 0)),
                  pl.BlockSpec((1, LANES), lambda i: (0, 0))],
        out_specs=[pl.BlockSpec((tm, LANES), lambda i: (i, 0)),
                   pl.BlockSpec((tm, LANES), lambda i: (i, 0))],
        out_shape=[jax.ShapeDtypeStruct((rows, LANES), I32),
                   jax.ShapeDtypeStruct((rows, LANES), F32)],
        compiler_params=_cparams(("arbitrary",), 40),
        name="router",
    )(h, wp, bp)


def _moe_tiles(n_assign):
    s_max = n_assign // MOE_NMAX + N_EXPERTS
    return s_max, s_max * MOE_NMAX


def _route(top_idx):
    n = top_idx.size
    s_max, rows_tot = _moe_tiles(n)
    e_flat = top_idx.reshape(-1)
    iota = jnp.arange(n, dtype=I32)
    _, order = lax.sort((e_flat, iota), num_keys=1, is_stable=True)
    _, rank_all = lax.sort((order, iota), num_keys=1)
    counts = jnp.sum((e_flat[:, None] == jnp.arange(N_EXPERTS, dtype=I32)[None, :]).astype(I32), axis=0)
    start = jnp.cumsum(counts) - counts
    n_st = (counts + MOE_NMAX - 1) // MOE_NMAX
    share = (counts + jnp.maximum(n_st, 1) - 1) // jnp.maximum(n_st, 1)
    q = jnp.maximum((share + MOE_R - 1) // MOE_R * MOE_R, MOE_R)
    st_end = jnp.cumsum(n_st)
    st_start = st_end - n_st
    n_live = st_end[-1]
    s_ids = jnp.arange(s_max, dtype=I32)
    st_e = jnp.minimum(jnp.searchsorted(st_end, s_ids, side="right"), N_EXPERTS - 1).astype(I32)
    t_in_e = s_ids - st_start[st_e]
    st_rows = jnp.clip(counts[st_e] - t_in_e * q[st_e], 0, q[st_e])
    st_rows = jnp.where(s_ids < n_live, st_rows, 0).astype(I32)
    st_src = (start[st_e] + t_in_e * q[st_e]).astype(I32)
    rank = rank_all - start[e_flat]
    pos = ((st_start[e_flat] + rank // q[e_flat]) * MOE_NMAX + rank % q[e_flat]).astype(I32)
    return dict(st_e=st_e, st_rows=st_rows, st_src=st_src, n_live=n_live.reshape(1).astype(I32),
                tok_sorted=(order // TOP_K).astype(I32), pos=pos)


def _gather_kernel(st_n, st_src, nblk_ref, tok_sorted, h_hbm, o_ref, buf, sem):
    i = pl.program_id(0)
    per = MOE_NMAX // GATHER_ROWS
    nblk = nblk_ref[0]
    slot = i % 2
    n_assign = tok_sorted.shape[0]

    def dma_groups(blk):
        left = st_n[blk // per] - (blk % per) * GATHER_ROWS
        rows = jnp.clip((left + MOE_R - 1) // MOE_R * MOE_R, 0, GATHER_ROWS)
        return rows // DMA_UNROLL

    def row_copy(tok, k, s):
        return pltpu.make_async_copy(h_hbm.at[pl.ds(tok, 1), :], buf.at[s, pl.ds(k, 1), :], sem.at[s])

    def issue(blk, s):
        base = st_src[blk // per] + (blk % per) * GATHER_ROWS

        def body(g, c):
            for u in range(DMA_UNROLL):
                k = g * DMA_UNROLL + u
                row_copy(tok_sorted[jnp.minimum(base + k, n_assign - 1)], k, s).start()
            return c
        lax.fori_loop(0, dma_groups(blk), body, 0)

    @pl.when(i == 0)
    def _():
        buf[...] = jnp.zeros(buf.shape, buf.dtype)
        issue(0, 0)

    nxt = jnp.minimum(i + 1, nblk - 1)

    @pl.when(i + 1 < nblk)
    def _():
        issue(nxt, 1 - slot)

    @pl.when(dma_groups(i) > 0)
    def _():
        def wait(g, c):
            for u in range(DMA_UNROLL):
                row_copy(0, g * DMA_UNROLL + u, slot).wait()
            return c
        lax.fori_loop(0, dma_groups(i), wait, 0)
        o_ref[...] = buf[slot].astype(o_ref.dtype)


def _gather(h, route, rows_tot):
    d = h.shape[1]
    per = MOE_NMAX // GATHER_ROWS
    nblk = route["n_live"] * per
    return pl.pallas_call(
        _gather_kernel,
        grid_spec=pltpu.PrefetchScalarGridSpec(
            num_scalar_prefetch=4,
            grid=(nblk[0],),
            in_specs=[pl.BlockSpec(memory_space=pl.ANY)],
            out_specs=pl.BlockSpec((GATHER_ROWS, d), lambda i, *_: (i, 0)),
            scratch_shapes=[pltpu.VMEM((2, GATHER_ROWS, d), F32), pltpu.SemaphoreType.DMA((2,))]),
        out_shape=jax.ShapeDtypeStruct((rows_tot, d), BF16),
        compiler_params=_cparams(("arbitrary",), 32),
        name="moe_gather",
    )(route["st_rows"], route["st_src"], nblk, route["tok_sorted"], h)


def _row_passes(cnt):
    return (cnt + MOE_R - 1) // MOE_R


def _for_row_count(cnt, fn):
    npass = _row_passes(cnt)
    for k in range(1, MOE_NMAX // MOE_R + 1):
        pl.when(npass == k)(functools.partial(fn, k * MOE_R))


def _moe_up_kernel(st_e, st_n, xs_ref, wg_ref, wu_ref, bg_ref, bu_ref, o_ref):
    def compute(rows):
        x = xs_ref[0:rows, :]
        g = jnp.dot(x, wg_ref[0].astype(BF16), preferred_element_type=F32) + bg_ref[0]
        u = jnp.dot(x, wu_ref[0].astype(BF16), preferred_element_type=F32) + bu_ref[0]
        g = jnp.minimum(g, SWIGLU_LIMIT)
        u = jnp.clip(u, -SWIGLU_LIMIT, SWIGLU_LIMIT)
        o_ref[0:rows, :] = ((u + 1.0) * (g * jax.nn.sigmoid(SWIGLU_ALPHA * g))).astype(o_ref.dtype)

    _for_row_count(st_n[pl.program_id(0)], compute)


def _moe_down_kernel(st_e, st_n, a_ref, wd_ref, bd_ref, o_ref):
    def compute(rows):
        o_ref[0:rows, :] = jnp.dot(a_ref[0:rows, :], wd_ref[0].astype(BF16),
                                   preferred_element_type=F32) + bd_ref[0]

    _for_row_count(st_n[pl.program_id(0)], compute)


def _w_cols(off):
    return lambda s, j, st_e, st_n: (st_e[s], 0, off + j)


def _moe_up(xs, route, w_gate_up, b_gate_up):
    rows_tot, d = xs.shape
    nj = D_EXPERT // MOE_TN_UP
    bias = b_gate_up.reshape(N_EXPERTS, 1, 2 * D_EXPERT)
    return pl.pallas_call(
        _moe_up_kernel,
        grid_spec=pltpu.PrefetchScalarGridSpec(
            num_scalar_prefetch=2,
            grid=(route["n_live"][0], nj),
            in_specs=[pl.BlockSpec((MOE_NMAX, d), lambda s, j, st_e, st_n: (s, 0)),
                      pl.BlockSpec((1, d, MOE_TN_UP), _w_cols(0)),
                      pl.BlockSpec((1, d, MOE_TN_UP), _w_cols(nj)),
                      pl.BlockSpec((1, 1, MOE_TN_UP), _w_cols(0)),
                      pl.BlockSpec((1, 1, MOE_TN_UP), _w_cols(nj))],
            out_specs=pl.BlockSpec((MOE_NMAX, MOE_TN_UP), lambda s, j, st_e, st_n: (s, j))),
        out_shape=jax.ShapeDtypeStruct((rows_tot, D_EXPERT), BF16),
        compiler_params=_cparams(("arbitrary", "arbitrary"), 58),
        name="moe_up",
    )(route["st_e"], route["st_rows"], xs, w_gate_up, w_gate_up, bias, bias)


def _moe_down(act, route, w_down, b_down):
    rows_tot, f = act.shape
    d = w_down.shape[2]
    nj = d // MOE_TN_DOWN
    return pl.pallas_call(
        _moe_down_kernel,
        grid_spec=pltpu.PrefetchScalarGridSpec(
            num_scalar_prefetch=2,
            grid=(route["n_live"][0], nj),
            in_specs=[pl.BlockSpec((MOE_NMAX, f), lambda s, j, st_e, st_n: (s, 0)),
                      pl.BlockSpec((1, f, MOE_TN_DOWN), _w_cols(0)),
                      pl.BlockSpec((1, 1, MOE_TN_DOWN), _w_cols(0))],
            out_specs=pl.BlockSpec((MOE_NMAX, MOE_TN_DOWN), lambda s, j, st_e, st_n: (s, j))),
        out_shape=jax.ShapeDtypeStruct((rows_tot, d), F32),
        compiler_params=_cparams(("arbitrary", "arbitrary"), 58),
        name="moe_down",
    )(route["st_e"], route["st_rows"], act, w_down, b_down.reshape(N_EXPERTS, 1, d))


def _combine_kernel(pos_ref, posn_ref, ys_hbm, x_ref, gate_ref, g_ref, op_ref, os_ref, buf, sem,
                    *, n_blk, n_prompt_blk):
    i = pl.program_id(0)
    slot = i % 2
    n = COMBINE_TOK * TOP_K

    def row_copy(p, t, k, s):
        return pltpu.make_async_copy(ys_hbm.at[pl.ds(p, 1), :], buf.at[s, k, pl.ds(t, 1), :], sem.at[s])

    def issue(pos_smem, s):
        def body(t, c):
            for k in range(TOP_K):
                row_copy(pos_smem[t * TOP_K + k], t, k, s).start()
            return c
        lax.fori_loop(0, COMBINE_TOK, body, 0, unroll=DMA_UNROLL // TOP_K)

    @pl.when(i == 0)
    def _():
        issue(pos_ref, 0)

    @pl.when(i + 1 < n_blk)
    def _():
        issue(posn_ref, 1 - slot)

    def wait(t, c):
        for k in range(TOP_K):
            row_copy(0, t, k, slot).wait()
        return c

    lax.fori_loop(0, COMBINE_TOK, wait, 0, unroll=DMA_UNROLL // TOP_K)
    x = x_ref[...]
    for k in range(TOP_K):
        x = x + buf[slot, k] * gate_ref[:, k:k + 1]
    ms = jnp.mean(x * x, axis=-1, keepdims=True)
    y = x * lax.rsqrt(ms + NORM_EPS) * g_ref[...]

    @pl.when(i < n_prompt_blk)
    def _():
        op_ref[...] = y

    @pl.when(i >= n_prompt_blk)
    def _():
        os_ref[...] = y


def _combine(ys, pos, x, gates, g_final, rows_prompt):
    rows, d = x.shape
    n_blk = rows // COMBINE_TOK
    npb = rows_prompt // COMBINE_TOK
    nidx = COMBINE_TOK * TOP_K
    return pl.pallas_call(
        functools.partial(_combine_kernel, n_blk=n_blk, n_prompt_blk=npb),
        grid=(n_blk,),
        in_specs=[pl.BlockSpec((nidx,), lambda i: (i,), memory_space=pltpu.SMEM),
                  pl.BlockSpec((nidx,), lambda i: (jnp.minimum(i + 1, n_blk - 1),), memory_space=pltpu.SMEM),
                  pl.BlockSpec(memory_space=pl.ANY),
                  pl.BlockSpec((COMBINE_TOK, d), lambda i: (i, 0)),
                  pl.BlockSpec((COMBINE_TOK, LANES), lambda i: (i, 0)),
                  pl.BlockSpec((1, d), lambda i: (0, 0))],
        out_specs=[pl.BlockSpec((COMBINE_TOK, d), lambda i: (jnp.minimum(i, npb - 1), 0)),
                   pl.BlockSpec((COMBINE_TOK, d), lambda i: (jnp.maximum(i - npb, 0), 0))],
        out_shape=[jax.ShapeDtypeStruct((rows_prompt, d), F32),
                   jax.ShapeDtypeStruct((rows - rows_prompt, d), F32)],
        scratch_shapes=[pltpu.VMEM((2, TOP_K, COMBINE_TOK, d), F32), pltpu.SemaphoreType.DMA((2,))],
        compiler_params=_cparams(("arbitrary",), 32),
        name="moe_combine",
    )(pos, pos, ys, x, gates, g_final.reshape(1, d))


def _tail_rows(z, batch, seq, nrows, c0, c1):
    return jnp.stack([lax.slice(z, ((b + 1) * seq - nrows, c0), ((b + 1) * seq, c1)) for b in range(batch)])


def kernel(x_prompt, x_sample, cache_band_k, cache_band_v, state_conv, cache_mem_k, cache_mem_v, mem_prompt, norm_mix, w_in, rel_table, conv_w, conv_b, conv_ln_g, conv_ln_b, grp_g_att, grp_g_conv, w_out, norm_mem_q, norm_mem_kv, w_mem_q, w_mem_k, w_mem_v, w_mem_o, norm_ffn, w_router, b_router, w_gate_up, b_gate_up, w_down, b_down, norm_final):
    depth = norm_mix.shape[0]
    assert depth == 1
    bp, sp, d = x_prompt.shape
    bs, ss, _ = x_sample.shape
    tp, ts = bp * sp, bs * ss
    t_all = tp + ts
    assert d == D_MODEL and sp % QB == 0 and sp % CONV_TB == 0 and tp % MM_ROWS == 0
    assert sp >= BAND_WINDOW and ss == CHUNK and t_all % MM_ROWS == 0
    assert tp % COMBINE_TOK == 0 and ts % COMBINE_TOK == 0
    l = 0

    x = (x_prompt.reshape(tp, d), x_sample.reshape(ts, d))

    h = _rmsnorm(x, norm_mix[l], BF16)
    z = _matmul([h], w_in[l].astype(BF16), None, F32, 1024, "in_proj")
    w_rows = _rel_rows(rel_table[l])
    att_p = _attn_prompt(z, w_rows, grp_g_att[l], bp, sp)
    ck = cache_band_k[l].reshape(bs, -1, HEAD_DIM)
    cv = cache_band_v[l].reshape(bs, -1, HEAD_DIM)
    att_s = _attn_sample(z, tp, ck, cv, w_rows, grp_g_att[l], bs, ss)
    conv_params = (conv_w[l], conv_b[l].reshape(1, -1), conv_ln_g[l].reshape(1, -1),
                   conv_ln_b[l].reshape(1, -1), grp_g_conv[l].reshape(1, -1))
    cnv_p, cs_p = _conv_prompt(z, conv_params, bp, sp)
    past = jnp.pad(state_conv[l], ((0, 0), (CONV_HALO - (CONV_K - 1), 0), (0, 0)))
    cnv_s, cs_s = _conv_sample(z, tp, past, conv_params, bs, ss)
    x1 = _matmul([(att_p, att_s), (cnv_p, cnv_s)], w_out[l].astype(BF16), x, F32, 256, "out_proj")

    hd = (ATT_HEADS, HEAD_DIM)
    bk_p = _tail_rows(z, bp, sp, BAND_WINDOW, ATT_WIDTH, 2 * ATT_WIDTH).reshape(1, bp, BAND_WINDOW, *hd)
    bv_p = _tail_rows(z, bp, sp, BAND_WINDOW, 2 * ATT_WIDTH, 3 * ATT_WIDTH).reshape(1, bp, BAND_WINDOW, *hd)
    bk_s = lax.slice(z, (tp, ATT_WIDTH), (t_all, 2 * ATT_WIDTH)).reshape(1, bs, ss, *hd)
    bv_s = lax.slice(z, (tp, 2 * ATT_WIDTH), (t_all, 3 * ATT_WIDTH)).reshape(1, bs, ss, *hd)
    keep = CONV_HALO - (CONV_K - 1)
    cs_p = cs_p[:, keep:][None]
    cs_s = cs_s[:, keep:][None]

    nm = mem_prompt.shape[1]
    mem_n = _rmsnorm(mem_prompt.reshape(bp * nm, d), norm_mem_kv[l], BF16)
    w_mkv = jnp.concatenate([w_mem_k[l], w_mem_v[l]], axis=1).astype(BF16)
    mkv = _matmul([mem_n], w_mkv, None, F32, 2 * MEM_WIDTH, "mem_kv").reshape(bp, nm, 2 * MEM_WIDTH)
    mk_p = mkv[:, :, :MEM_WIDTH].reshape(1, bp, nm, MEM_HEADS, HEAD_DIM)
    mv_p = mkv[:, :, MEM_WIDTH:].reshape(1, bp, nm, MEM_HEADS, HEAD_DIM)
    hq = _rmsnorm(x1, norm_mem_q[l], BF16)
    q = _matmul([hq], w_mem_q[l].astype(BF16), None, F32, MEM_WIDTH, "mem_q")
    mo_p = _memattn(q, 0, mkv, mkv, 0, 1, bp, sp, 512, "memattn_prompt")
    cmk = cache_mem_k[l].reshape(bs, nm, MEM_WIDTH)
    cmv = cache_mem_v[l].reshape(bs, nm, MEM_WIDTH)
    mo_s = _memattn(q, tp, cmk, cmv, 0, 0, bs, ss, ss, "memattn_sample")
    x2 = _matmul([(mo_p, mo_s)], w_mem_o[l].astype(BF16), x1, F32, 512, "mem_o")

    hf = _rmsnorm(x2, norm_ffn[l], F32)
    top_idx, gates = _router(hf, w_router[l], b_router[l])
    route = _route(top_idx[:, :TOP_K])
    _, rows_tot = _moe_tiles(t_all * TOP_K)
    xs = _gather(hf, route, rows_tot)
    act = _moe_up(xs, route, w_gate_up[l], b_gate_up[l])
    ys = _moe_down(act, route, w_down[l], b_down[l])
    y_p, y_s = _combine(ys, route["pos"], x2, gates, norm_final, tp)

    return (y_p.reshape(bp, sp, d), y_s.reshape(bs, ss, d), bk_p, bv_p, cs_p, mk_p, mv_p, bk_s, bv_s, cs_s)
```

```python
import functools

import jax
import jax.numpy as jnp
from jax import lax
from jax.experimental import pallas as pl
from jax.experimental.pallas import tpu as pltpu

F32 = jnp.float32
BF16 = jnp.bfloat16
I32 = jnp.int32

D_MODEL = 4096
CHUNK = 64
LEFT_CHUNKS = 8
BAND_WINDOW = LEFT_CHUNKS * CHUNK
HEAD_DIM = 128
ATT_WIDTH = 2048
CONV_WIDTH = 2048
ATT_HEADS = 16
IN_WIDTH = 3 * ATT_WIDTH + 2 * CONV_WIDTH
MAX_REL = 256
CONV_K = 31
MEM_TOKENS = 256
MEM_HEADS = 4
MEM_WIDTH = 512
N_EXPERTS = 32
TOP_K = 4
D_EXPERT = 4096
SWIGLU_LIMIT = 7.0
SWIGLU_ALPHA = 1.702
NORM_EPS = 1e-5
NEG_INF = -1e30

LANES = 128
SUBLANES = 8

NORM_ROWS = 512
MM_ROWS = 1024
ATTN_HG = 8
QB = 256
KWIN = QB + BAND_WINDOW
ROLL_W = 1024
SKEYS = 640
CONV_TB = 256
CONV_HALO = 32
CONV_CW = 512
MOE_R = 128
MOE_NMAX = 1536
MOE_TN_UP = 256
MOE_TN_DOWN = 512
GATHER_ROWS = 256
COMBINE_TOK = 128
DMA_UNROLL = 8


def _cparams(sem, vmem_mib):
    return pltpu.CompilerParams(dimension_semantics=sem, vmem_limit_bytes=vmem_mib * 2**20)


def _pair_specs(pair, tm, width, col_map=None):
    first, second = pair
    assert first.shape[0] % tm == 0 and second.shape[0] % tm == 0
    nf = first.shape[0] // tm
    col = col_map if col_map is not None else (lambda *ij: 0)
    specs = [pl.BlockSpec((tm, width), lambda *ij: (jnp.minimum(ij[0], nf - 1), col(*ij))),
             pl.BlockSpec((tm, width), lambda *ij: (jnp.maximum(ij[0] - nf, 0), col(*ij)))]
    return specs, nf


def _pair_value(refs, nf):
    return jnp.where(pl.program_id(0) < nf, refs[0][...], refs[1][...])


def _rmsnorm_kernel(*refs, nf):
    g_ref, o_ref = refs[-2], refs[-1]
    x = refs[0][...] if nf is None else _pair_value(refs[:2], nf)
    ms = jnp.mean(x * x, axis=-1, keepdims=True)
    o_ref[...] = (x * lax.rsqrt(ms + NORM_EPS) * g_ref[...]).astype(o_ref.dtype)


def _rmsnorm(x, g, out_dtype):
    if isinstance(x, tuple):
        rows, d = x[0].shape[0] + x[1].shape[0], x[0].shape[1]
        tm = NORM_ROWS // 2
        x_specs, nf = _pair_specs(x, tm, d)
        xs = list(x)
    else:
        rows, d = x.shape
        tm = min(NORM_ROWS, rows)
        x_specs, nf, xs = [pl.BlockSpec((tm, d), lambda i: (i, 0))], None, [x]
    return pl.pallas_call(
        functools.partial(_rmsnorm_kernel, nf=nf),
        grid=(rows // tm,),
        in_specs=x_specs + [pl.BlockSpec((1, d), lambda i: (0, 0))],
        out_specs=pl.BlockSpec((tm, d), lambda i: (i, 0)),
        out_shape=jax.ShapeDtypeStruct((rows, d), out_dtype),
        compiler_params=_cparams(("arbitrary",), 48),
        name="rmsnorm",
    )(*xs, g.reshape(1, d))


def _mm_kernel(*refs, layout, nf):
    a_counts, res_count = layout[:-1], layout[-1]
    pos = 0
    a_vals = []
    for c in a_counts:
        a_vals.append(refs[pos][...] if c == 1 else _pair_value(refs[pos:pos + 2], nf))
        pos += c
    b_refs = refs[pos:pos + len(a_counts)]
    pos += len(a_counts)
    acc = jnp.dot(a_vals[0], b_refs[0][...], preferred_element_type=F32)
    for k in range(1, len(a_vals)):
        acc = acc + jnp.dot(a_vals[k], b_refs[k][...], preferred_element_type=F32)
    if res_count:
        acc = acc + (refs[pos][...] if res_count == 1 else _pair_value(refs[pos:pos + 2], nf))
    refs[-1][...] = acc.astype(refs[-1].dtype)


def _rows_of(x):
    return x[0].shape[0] + x[1].shape[0] if isinstance(x, tuple) else x.shape[0]


def _matmul(a_parts, w, res, out_dtype, tn, name, tm=MM_ROWS):
    m = _rows_of(a_parts[0])
    kk = (a_parts[0][0] if isinstance(a_parts[0], tuple) else a_parts[0]).shape[1]
    n = w.shape[1]
    tm = min(tm, m)
    in_specs, args, layout, nf = [], [], [], None
    for a in a_parts:
        if isinstance(a, tuple):
            specs, nf = _pair_specs(a, tm, kk)
            in_specs += specs
            args += list(a)
            layout.append(2)
        else:
            in_specs.append(pl.BlockSpec((tm, kk), lambda i, j: (i, 0)))
            args.append(a)
            layout.append(1)
    in_specs += [pl.BlockSpec((kk, tn), functools.partial(lambda i, j, p: (p, j), p=p))
                 for p in range(len(a_parts))]
    args += [w] * len(a_parts)
    if isinstance(res, tuple):
        specs, nf = _pair_specs(res, tm, tn, col_map=lambda i, j: j)
        in_specs += specs
        args += list(res)
        layout.append(2)
    elif res is not None:
        in_specs.append(pl.BlockSpec((tm, tn), lambda i, j: (i, j)))
        args.append(res)
        layout.append(1)
    else:
        layout.append(0)
    return pl.pallas_call(
        functools.partial(_mm_kernel, layout=tuple(layout), nf=nf),
        grid=(m // tm, n // tn),
        in_specs=in_specs,
        out_specs=pl.BlockSpec((tm, tn), lambda i, j: (i, j)),
        out_shape=jax.ShapeDtypeStruct((m, n), out_dtype),
        compiler_params=_cparams(("arbitrary", "arbitrary"), 56),
        name=name,
    )(*args)


def _rel_rows(table):
    m = jnp.arange(ROLL_W)
    m = jnp.where(m < KWIN, m, m - ROLL_W)
    idx = jnp.clip(BAND_WINDOW - m, -MAX_REL, MAX_REL) + MAX_REL
    return table[:, idx].astype(F32)


def _toeplitz_bias(w_row, rows):
    wb = jnp.broadcast_to(w_row, (rows, ROLL_W))
    return pltpu.roll(wb, 0, 1, stride=1, stride_axis=0)


def _group_rmsnorm_store(o, g_ref, o_ref):
    ms = jnp.mean(o * o, axis=-1, keepdims=True)
    o_ref[...] = (o * lax.rsqrt(ms + NORM_EPS) * g_ref[...]).astype(o_ref.dtype)


def _attn_prompt_kernel(w_ref, q_ref, k0_ref, k1_ref, k2_ref, v0_ref, v1_ref, v2_ref, g_ref,
                        o_ref, bias_sc, o_sc):
    i = pl.program_id(1)
    hg = pl.program_id(2)
    first = (pl.program_id(0) == 0) & (i == 0) & (hg == 0)

    @pl.when(first)
    def _():
        r = lax.broadcasted_iota(I32, (QB, KWIN), 0) // CHUNK
        c = lax.broadcasted_iota(I32, (QB, KWIN), 1) // CHUNK
        band = (c >= r) & (c <= r + LEFT_CHUNKS)
        for h in range(ATT_HEADS):
            t = _toeplitz_bias(w_ref[h:h + 1, :], QB)
            bias_sc[h] = jnp.where(band, t[:, :KWIN], NEG_INF)

    kk = jnp.concatenate([r[...].astype(BF16) for r in (k0_ref, k1_ref, k2_ref)], axis=0)
    vv = jnp.concatenate([r[...].astype(BF16) for r in (v0_ref, v1_ref, v2_ref)], axis=0)
    col = lax.broadcasted_iota(I32, (QB, KWIN), 1)
    in_seq = col >= (BAND_WINDOW // QB - i) * QB
    scale = HEAD_DIM ** -0.5
    for hh in range(ATTN_HG):
        hs = slice(hh * HEAD_DIM, (hh + 1) * HEAD_DIM)
        qh = q_ref[:, hs].astype(BF16)
        s = lax.dot_general(qh, kk[:, hs], (((1,), (1,)), ((), ())),
                            preferred_element_type=F32) * scale + bias_sc[hg * ATTN_HG + hh]
        s = jnp.where(in_seq, s, NEG_INF)
        m = jnp.max(s, axis=-1, keepdims=True)
        p = jnp.exp(s - m)
        l = jnp.sum(p, axis=-1, keepdims=True)
        o = jnp.dot(p.astype(BF16), vv[:, hs], preferred_element_type=F32)
        o_sc[hg, :, hs] = o / l

    @pl.when(hg == ATT_HEADS // ATTN_HG - 1)
    def _():
        o = jnp.concatenate([o_sc[g] for g in range(ATT_HEADS // ATTN_HG)], axis=1)
        _group_rmsnorm_store(o, g_ref, o_ref)


def _attn_prompt(z, w_rows, g_att, batch, seq):
    nqb = seq // QB
    back = BAND_WINDOW // QB
    gw = ATTN_HG * HEAD_DIM
    ngrp = ATT_HEADS // ATTN_HG

    def kv_spec(col0, d):
        return pl.BlockSpec((QB, gw),
                            lambda b, i, g: (b * nqb + jnp.maximum(i - d, 0), col0 + g))

    kc, vc = ATT_WIDTH // gw, 2 * ATT_WIDTH // gw
    return pl.pallas_call(
        _attn_prompt_kernel,
        grid=(batch, nqb, ngrp),
        in_specs=[pl.BlockSpec((ATT_HEADS, ROLL_W), lambda b, i, g: (0, 0)),
                  pl.BlockSpec((QB, gw), lambda b, i, g: (b * nqb + i, g)),
                  kv_spec(kc, back), kv_spec(kc, back - 1), kv_spec(kc, 0),
                  kv_spec(vc, back), kv_spec(vc, back - 1), kv_spec(vc, 0),
                  pl.BlockSpec((1, ATT_WIDTH), lambda b, i, g: (0, 0))],
        out_specs=pl.BlockSpec((QB, ATT_WIDTH), lambda b, i, g: (b * nqb + i, 0)),
        out_shape=jax.ShapeDtypeStruct((batch * seq, ATT_WIDTH), BF16),
        scratch_shapes=[pltpu.VMEM((ATT_HEADS, QB, KWIN), F32),
                        pltpu.VMEM((ngrp, QB, gw), F32)],
        compiler_params=_cparams(("arbitrary", "arbitrary", "arbitrary"), 48),
        name="attn_prompt",
    )(w_rows, z, z, z, z, z, z, z, g_att.reshape(1, ATT_WIDTH))


def _attn_sample_kernel(w_ref, q_ref, kn_ref, vn_ref, kc_ref, vc_ref, g_ref, o_ref, bias_sc, o_sc,
                        *, past, new):
    @pl.when(pl.program_id(0) == 0)
    def _():
        c = lax.broadcasted_iota(I32, (new, SKEYS), 1)
        for h in range(ATT_HEADS):
            t = _toeplitz_bias(w_ref[h:h + 1, :], new)
            bias_sc[h] = jnp.where(c < past + new, t[:, :SKEYS], NEG_INF)

    pad = jnp.zeros((SKEYS - past - new, HEAD_DIM), BF16)
    scale = HEAD_DIM ** -0.5
    for h in range(ATT_HEADS):
        hs = slice(h * HEAD_DIM, (h + 1) * HEAD_DIM)
        head_rows = pl.ds(h, past, stride=ATT_HEADS)
        kh = jnp.concatenate([kc_ref[0, head_rows, :].astype(BF16), kn_ref[:, hs].astype(BF16), pad], axis=0)
        vh = jnp.concatenate([vc_ref[0, head_rows, :].astype(BF16), vn_ref[:, hs].astype(BF16), pad], axis=0)
        qh = q_ref[:, hs].astype(BF16)
        s = lax.dot_general(qh, kh, (((1,), (1,)), ((), ())),
                            preferred_element_type=F32) * scale + bias_sc[h]
        m = jnp.max(s, axis=-1, keepdims=True)
        p = jnp.exp(s - m)
        l = jnp.sum(p, axis=-1, keepdims=True)
        o = jnp.dot(p.astype(BF16), vh, preferred_element_type=F32)
        o_sc[:, hs] = o / l
    _group_rmsnorm_store(o_sc[...], g_ref, o_ref)


def _attn_sample(z, row0, cache_k, cache_v, w_rows, g_att, batch, new):
    past = cache_k.shape[1] // ATT_HEADS
    assert past == BAND_WINDOW and past + new <= SKEYS and row0 % new == 0
    blk0 = row0 // new
    return pl.pallas_call(
        functools.partial(_attn_sample_kernel, past=past, new=new),
        grid=(batch,),
        in_specs=[pl.BlockSpec((ATT_HEADS, ROLL_W), lambda b: (0, 0)),
                  pl.BlockSpec((new, ATT_WIDTH), lambda b: (blk0 + b, 0)),
                  pl.BlockSpec((new, ATT_WIDTH), lambda b: (blk0 + b, 1)),
                  pl.BlockSpec((new, ATT_WIDTH), lambda b: (blk0 + b, 2)),
                  pl.BlockSpec((1, past * ATT_HEADS, HEAD_DIM), lambda b: (b, 0, 0)),
                  pl.BlockSpec((1, past * ATT_HEADS, HEAD_DIM), lambda b: (b, 0, 0)),
                  pl.BlockSpec((1, ATT_WIDTH), lambda b: (0, 0))],
        out_specs=pl.BlockSpec((new, ATT_WIDTH), lambda b: (b, 0)),
        out_shape=jax.ShapeDtypeStruct((batch * new, ATT_WIDTH), BF16),
        scratch_shapes=[pltpu.VMEM((ATT_HEADS, new, SKEYS), F32),
                        pltpu.VMEM((new, ATT_WIDTH), F32)],
        compiler_params=_cparams(("arbitrary",), 48),
        name="attn_sample",
    )(w_rows, z, z, z, cache_k, cache_v, g_att.reshape(1, ATT_WIDTH))


def _conv_tail(u, hist, w_ref, cb_ref, lg_ref, lb_ref, gg_ref, o_ref, st_ref, u_sc, y_sc, s_sc, tb):
    u_sc[0:CONV_HALO, :] = hist
    u_sc[CONV_HALO:CONV_HALO + tb, :] = u
    u_sc[CONV_HALO + tb:CONV_HALO + tb + SUBLANES, :] = jnp.zeros((SUBLANES, CONV_WIDTH), F32)
    st_ref[0] = u[tb - CONV_HALO:, :]
    off = CONV_HALO - (CONV_K - 1)
    srows = tb + SUBLANES
    for cc in range(CONV_WIDTH // CONV_CW):
        cs = slice(cc * CONV_CW, (cc + 1) * CONV_CW)
        for p in range(SUBLANES):
            acc = None
            for j in range(CONV_K):
                if (off + j) % SUBLANES != p:
                    continue
                a = off + j - p
                term = u_sc[a:a + srows, cs] * w_ref[j:j + 1, cs]
                acc = term if acc is None else acc + term
            s_sc[p] = acc
        y = s_sc[0, 0:tb, :]
        for p in range(1, SUBLANES):
            y = y + s_sc[p, p:p + tb, :]
        y_sc[:, cs] = y + cb_ref[:, cs]
    y = y_sc[...]
    mu = jnp.mean(y, axis=-1, keepdims=True)
    yc = y - mu
    y = yc * lax.rsqrt(jnp.mean(yc * yc, axis=-1, keepdims=True) + NORM_EPS)
    y = y * lg_ref[...] + lb_ref[...]
    y = y * jax.nn.sigmoid(y)
    _group_rmsnorm_store(y, gg_ref, o_ref)


def _conv_prompt_kernel(cv_ref, cg_ref, pv_ref, pg_ref, w_ref, cb_ref, lg_ref, lb_ref, gg_ref,
                        o_ref, st_ref, u_sc, y_sc, s_sc):
    u = cv_ref[...] * jax.nn.sigmoid(cg_ref[...])
    hist = pv_ref[...] * jax.nn.sigmoid(pg_ref[...])
    hist = jnp.where(pl.program_id(1) > 0, hist, 0.0)
    _conv_tail(u, hist, w_ref, cb_ref, lg_ref, lb_ref, gg_ref, o_ref, st_ref, u_sc, y_sc, s_sc, CONV_TB)


def _conv_sample_kernel(cv_ref, cg_ref, past_ref, w_ref, cb_ref, lg_ref, lb_ref, gg_ref,
                        o_ref, st_ref, u_sc, y_sc, s_sc, *, tb):
    u = cv_ref[...] * jax.nn.sigmoid(cg_ref[...])
    _conv_tail(u, past_ref[0], w_ref, cb_ref, lg_ref, lb_ref, gg_ref, o_ref, st_ref, u_sc, y_sc, s_sc, tb)


def _conv_scratch(tb):
    return [pltpu.VMEM((CONV_HALO + tb + SUBLANES, CONV_WIDTH), F32),
            pltpu.VMEM((tb, CONV_WIDTH), F32),
            pltpu.VMEM((SUBLANES, tb + SUBLANES, CONV_CW), F32)]


def _conv_param_specs(nidx):
    zero = (lambda b, t: (0, 0)) if nidx == 2 else (lambda b: (0, 0))
    return [pl.BlockSpec((CONV_K, CONV_WIDTH), zero)] + [pl.BlockSpec((1, CONV_WIDTH), zero)] * 4


def _conv_prompt(z, params, batch, seq):
    ntb = seq // CONV_TB
    per = CONV_TB // CONV_HALO
    cv_blk, cg_blk = 3 * ATT_WIDTH // CONV_WIDTH, 3 * ATT_WIDTH // CONV_WIDTH + 1

    def cur(col):
        return pl.BlockSpec((CONV_TB, CONV_WIDTH), lambda b, t: (b * ntb + t, col))

    def prev(col):
        return pl.BlockSpec((CONV_HALO, CONV_WIDTH),
                            lambda b, t: (jnp.maximum((b * ntb + t) * per - 1, 0), col))

    return pl.pallas_call(
        _conv_prompt_kernel,
        grid=(batch, ntb),
        in_specs=[cur(cv_blk), cur(cg_blk), prev(cv_blk), prev(cg_blk)] + _conv_param_specs(2),
        out_specs=[pl.BlockSpec((CONV_TB, CONV_WIDTH), lambda b, t: (b * ntb + t, 0)),
                   pl.BlockSpec((1, CONV_HALO, CONV_WIDTH), lambda b, t: (b, 0, 0))],
        out_shape=[jax.ShapeDtypeStruct((batch * seq, CONV_WIDTH), BF16),
                   jax.ShapeDtypeStruct((batch, CONV_HALO, CONV_WIDTH), F32)],
        scratch_shapes=_conv_scratch(CONV_TB),
        compiler_params=_cparams(("arbitrary", "arbitrary"), 48),
        name="conv_prompt",
    )(z, z, z, z, *params)


def _conv_sample(z, row0, past, params, batch, new):
    assert row0 % new == 0 and new >= CONV_HALO
    blk0 = row0 // new
    cv_blk, cg_blk = 3 * ATT_WIDTH // CONV_WIDTH, 3 * ATT_WIDTH // CONV_WIDTH + 1
    return pl.pallas_call(
        functools.partial(_conv_sample_kernel, tb=new),
        grid=(batch,),
        in_specs=[pl.BlockSpec((new, CONV_WIDTH), lambda b: (blk0 + b, cv_blk)),
                  pl.BlockSpec((new, CONV_WIDTH), lambda b: (blk0 + b, cg_blk)),
                  pl.BlockSpec((1, CONV_HALO, CONV_WIDTH), lambda b: (b, 0, 0))] + _conv_param_specs(1),
        out_specs=[pl.BlockSpec((new, CONV_WIDTH), lambda b: (b, 0)),
                   pl.BlockSpec((1, CONV_HALO, CONV_WIDTH), lambda b: (b, 0, 0))],
        out_shape=[jax.ShapeDtypeStruct((batch * new, CONV_WIDTH), BF16),
                   jax.ShapeDtypeStruct((batch, CONV_HALO, CONV_WIDTH), F32)],
        scratch_shapes=_conv_scratch(new),
        compiler_params=_cparams(("arbitrary",), 32),
        name="conv_sample",
    )(z, z, past, *params)


def _memattn_kernel(q_ref, mk_ref, mv_ref, o_ref):
    mk = mk_ref[0].astype(BF16)
    mv = mv_ref[0].astype(BF16)
    scale = HEAD_DIM ** -0.5
    for h in range(MEM_HEADS):
        hs = slice(h * HEAD_DIM, (h + 1) * HEAD_DIM)
        qh = q_ref[:, hs].astype(BF16)
        s = lax.dot_general(qh, mk[:, hs], (((1,), (1,)), ((), ())),
                            preferred_element_type=F32) * scale
        m = jnp.max(s, axis=-1, keepdims=True)
        p = jnp.exp(s - m)
        l = jnp.sum(p, axis=-1, keepdims=True)
        o = jnp.dot(p.astype(BF16), mv[:, hs], preferred_element_type=F32)
        o_ref[:, hs] = (o / l).astype(o_ref.dtype)


def _memattn(q, row0, mk, mv, k_col, v_col, batch, rows_per_batch, tb, name):
    nb = rows_per_batch // tb
    blk0 = row0 // tb
    return pl.pallas_call(
        _memattn_kernel,
        grid=(batch, nb),
        in_specs=[pl.BlockSpec((tb, MEM_WIDTH), lambda b, t: (blk0 + b * nb + t, 0)),
                  pl.BlockSpec((1, MEM_TOKENS, MEM_WIDTH), lambda b, t: (b, 0, k_col)),
                  pl.BlockSpec((1, MEM_TOKENS, MEM_WIDTH), lambda b, t: (b, 0, v_col))],
        out_specs=pl.BlockSpec((tb, MEM_WIDTH), lambda b, t: (b * nb + t, 0)),
        out_shape=jax.ShapeDtypeStruct((batch * rows_per_batch, MEM_WIDTH), BF16),
        compiler_params=_cparams(("arbitrary", "arbitrary"), 32),
        name=name,
    )(q, mk, mv)


def _router_kernel(h_ref, w_ref, b_ref, idx_ref, gate_ref):
    logits = jnp.dot(h_ref[...], w_ref[...], preferred_element_type=F32,
                     precision=lax.Precision.HIGHEST) + b_ref[...]
    lane = lax.broadcasted_iota(I32, logits.shape, 1)
    logits = jnp.where(lane < N_EXPERTS, logits, -jnp.inf)
    vals, idxs = [], []
    for _ in range(TOP_K):
        m = jnp.max(logits, axis=-1, keepdims=True)
        idx = jnp.min(jnp.where(logits == m, lane, LANES), axis=-1, keepdims=True)
        vals.append(m)
        idxs.append(idx)
        logits = jnp.where(lane == idx, -jnp.inf, logits)
    es = [jnp.exp(v - vals[0]) for v in vals]
    tot = es[0]
    for e in es[1:]:
        tot = tot + e
    idx_out = jnp.zeros(logits.shape, I32)
    gate_out = jnp.zeros(logits.shape, F32)
    for k in range(TOP_K):
        idx_out = jnp.where(lane == k, idxs[k], idx_out)
        gate_out = jnp.where(lane == k, es[k] / tot, gate_out)
    idx_ref[...] = idx_out
    gate_ref[...] = gate_out


def _router(h, w_router, b_router):
    rows, d = h.shape
    tm = NORM_ROWS
    wp = jnp.zeros((d, LANES), F32).at[:, :N_EXPERTS].set(w_router)
    bp = jnp.zeros((1, LANES), F32).at[0, :N_EXPERTS].set(b_router)
    return pl.pallas_call(
        _router_kernel,
        grid=(rows // tm,),
        in_specs=[pl.BlockSpec((tm, d), lambda i: (i, 0)),
                  pl.BlockSpec((d, LANES), lambda i: (0, 0)),
                  pl.BlockSpec((1, LANES), lambda i: (0, 0))],
        out_specs=[pl.BlockSpec((tm, LANES), lambda i: (i, 0)),
                   pl.BlockSpec((tm, LANES), lambda i: (i, 0))],
        out_shape=[jax.ShapeDtypeStruct((rows, LANES), I32),
                   jax.ShapeDtypeStruct((rows, LANES), F32)],
        compiler_params=_cparams(("arbitrary",), 40),
        name="router",
    )(h, wp, bp)


def _moe_tiles(n_assign):
    s_max = n_assign // MOE_NMAX + N_EXPERTS
    return s_max, s_max * MOE_NMAX


def _route(top_idx):
    n = top_idx.size
    s_max, rows_tot = _moe_tiles(n)
    e_flat = top_idx.reshape(-1)
    iota = jnp.arange(n, dtype=I32)
    _, order = lax.sort((e_flat, iota), num_keys=1, is_stable=True)
    _, rank_all = lax.sort((order, iota), num_keys=1)
    counts = jnp.sum((e_flat[:, None] == jnp.arange(N_EXPERTS, dtype=I32)[None, :]).astype(I32), axis=0)
    start = jnp.cumsum(counts) - counts
    n_st = (counts + MOE_NMAX - 1) // MOE_NMAX
    share = (counts + jnp.maximum(n_st, 1) - 1) // jnp.maximum(n_st, 1)
    q = jnp.maximum((share + MOE_R - 1) // MOE_R * MOE_R, MOE_R)
    st_end = jnp.cumsum(n_st)
    st_start = st_end - n_st
    n_live = st_end[-1]
    s_ids = jnp.arange(s_max, dtype=I32)
    st_e = jnp.minimum(jnp.searchsorted(st_end, s_ids, side="right"), N_EXPERTS - 1).astype(I32)
    t_in_e = s_ids - st_start[st_e]
    st_rows = jnp.clip(counts[st_e] - t_in_e * q[st_e], 0, q[st_e])
    st_rows = jnp.where(s_ids < n_live, st_rows, 0).astype(I32)
    st_src = (start[st_e] + t_in_e * q[st_e]).astype(I32)
    rank = rank_all - start[e_flat]
    pos = ((st_start[e_flat] + rank // q[e_flat]) * MOE_NMAX + rank % q[e_flat]).astype(I32)
    return dict(st_e=st_e, st_rows=st_rows, st_src=st_src, n_live=n_live.reshape(1).astype(I32),
                tok_sorted=(order // TOP_K).astype(I32), pos=pos)


def _gather_kernel(st_n, st_src, nblk_ref, tok_sorted, h_hbm, o_ref, buf, sem):
    i = pl.program_id(0)
    per = MOE_NMAX // GATHER_ROWS
    nblk = nblk_ref[0]
    slot = i % 2
    n_assign = tok_sorted.shape[0]

    def dma_groups(blk):
        left = st_n[blk // per] - (blk % per) * GATHER_ROWS
        rows = jnp.clip((left + MOE_R - 1) // MOE_R * MOE_R, 0, GATHER_ROWS)
        return rows // DMA_UNROLL

    def row_copy(tok, k, s):
        return pltpu.make_async_copy(h_hbm.at[pl.ds(tok, 1), :], buf.at[s, pl.ds(k, 1), :], sem.at[s])

    def issue(blk, s):
        base = st_src[blk // per] + (blk % per) * GATHER_ROWS

        def body(g, c):
            for u in range(DMA_UNROLL):
                k = g * DMA_UNROLL + u
                row_copy(tok_sorted[jnp.minimum(base + k, n_assign - 1)], k, s).start()
            return c
        lax.fori_loop(0, dma_groups(blk), body, 0)

    @pl.when(i == 0)
    def _():
        buf[...] = jnp.zeros(buf.shape, buf.dtype)
        issue(0, 0)

    nxt = jnp.minimum(i + 1, nblk - 1)

    @pl.when(i + 1 < nblk)
    def _():
        issue(nxt, 1 - slot)

    @pl.when(dma_groups(i) > 0)
    def _():
        def wait(g, c):
            for u in range(DMA_UNROLL):
                row_copy(0, g * DMA_UNROLL + u, slot).wait()
            return c
        lax.fori_loop(0, dma_groups(i), wait, 0)
        o_ref[...] = buf[slot].astype(o_ref.dtype)


def _gather(h, route, rows_tot):
    d = h.shape[1]
    per = MOE_NMAX // GATHER_ROWS
    nblk = route["n_live"] * per
    return pl.pallas_call(
        _gather_kernel,
        grid_spec=pltpu.PrefetchScalarGridSpec(
            num_scalar_prefetch=4,
            grid=(nblk[0],),
            in_specs=[pl.BlockSpec(memory_space=pl.ANY)],
            out_specs=pl.BlockSpec((GATHER_ROWS, d), lambda i, *_: (i, 0)),
            scratch_shapes=[pltpu.VMEM((2, GATHER_ROWS, d), F32), pltpu.SemaphoreType.DMA((2,))]),
        out_shape=jax.ShapeDtypeStruct((rows_tot, d), BF16),
        compiler_params=_cparams(("arbitrary",), 32),
        name="moe_gather",
    )(route["st_rows"], route["st_src"], nblk, route["tok_sorted"], h)


def _row_passes(cnt):
    return (cnt + MOE_R - 1) // MOE_R


def _for_row_count(cnt, fn):
    npass = _row_passes(cnt)
    for k in range(1, MOE_NMAX // MOE_R + 1):
        pl.when(npass == k)(functools.partial(fn, k * MOE_R))


def _bias_cols(b_ref, col0, width):
    return b_ref[0, :, pl.ds(pl.multiple_of(col0, LANES), width)]


def _moe_up_kernel(st_e, st_n, xs_ref, wg_ref, wu_ref, b_ref, o_ref):
    col0 = pl.program_id(1) * MOE_TN_UP

    def compute(rows):
        x = xs_ref[0:rows, :]
        g = jnp.dot(x, wg_ref[0].astype(BF16), preferred_element_type=F32) + _bias_cols(b_ref, col0, MOE_TN_UP)
        u = (jnp.dot(x, wu_ref[0].astype(BF16), preferred_element_type=F32)
             + _bias_cols(b_ref, D_EXPERT + col0, MOE_TN_UP))
        g = jnp.minimum(g, SWIGLU_LIMIT)
        u = jnp.clip(u, -SWIGLU_LIMIT, SWIGLU_LIMIT)
        o_ref[0:rows, :] = ((u + 1.0) * (g * jax.nn.sigmoid(SWIGLU_ALPHA * g))).astype(o_ref.dtype)

    _for_row_count(st_n[pl.program_id(0)], compute)


def _moe_down_kernel(st_e, st_n, a_ref, wd_ref, b_ref, o_ref):
    col0 = pl.program_id(1) * MOE_TN_DOWN

    def compute(rows):
        o_ref[0:rows, :] = jnp.dot(a_ref[0:rows, :], wd_ref[0].astype(BF16),
                                   preferred_element_type=F32) + _bias_cols(b_ref, col0, MOE_TN_DOWN)

    _for_row_count(st_n[pl.program_id(0)], compute)


def _w_cols(off):
    return lambda s, j, st_e, st_n: (st_e[s], 0, off + j)


def _moe_up(xs, route, w_gate_up, b_gate_up):
    rows_tot, d = xs.shape
    nj = D_EXPERT // MOE_TN_UP
    bias = b_gate_up.reshape(N_EXPERTS, 1, 2 * D_EXPERT)
    return pl.pallas_call(
        _moe_up_kernel,
        grid_spec=pltpu.PrefetchScalarGridSpec(
            num_scalar_prefetch=2,
            grid=(route["n_live"][0], nj),
            in_specs=[pl.BlockSpec((MOE_NMAX, d), lambda s, j, st_e, st_n: (s, 0)),
                      pl.BlockSpec((1, d, MOE_TN_UP), _w_cols(0)),
                      pl.BlockSpec((1, d, MOE_TN_UP), _w_cols(nj)),
                      pl.BlockSpec((1, 1, 2 * D_EXPERT), lambda s, j, st_e, st_n: (st_e[s], 0, 0))],
            out_specs=pl.BlockSpec((MOE_NMAX, MOE_TN_UP), lambda s, j, st_e, st_n: (s, j))),
        out_shape=jax.ShapeDtypeStruct((rows_tot, D_EXPERT), BF16),
        compiler_params=_cparams(("arbitrary", "arbitrary"), 58),
        name="moe_up",
    )(route["st_e"], route["st_rows"], xs, w_gate_up, w_gate_up, bias)


def _moe_down(act, route, w_down, b_down):
    rows_tot, f = act.shape
    d = w_down.shape[2]
    nj = d // MOE_TN_DOWN
    return pl.pallas_call(
        _moe_down_kernel,
        grid_spec=pltpu.PrefetchScalarGridSpec(
            num_scalar_prefetch=2,
            grid=(route["n_live"][0], nj),
            in_specs=[pl.BlockSpec((MOE_NMAX, f), lambda s, j, st_e, st_n: (s, 0)),
                      pl.BlockSpec((1, f, MOE_TN_DOWN), _w_cols(0)),
                      pl.BlockSpec((1, 1, d), lambda s, j, st_e, st_n: (st_e[s], 0, 0))],
            out_specs=pl.BlockSpec((MOE_NMAX, MOE_TN_DOWN), lambda s, j, st_e, st_n: (s, j))),
        out_shape=jax.ShapeDtypeStruct((rows_tot, d), F32),
        compiler_params=_cparams(("arbitrary", "arbitrary"), 58),
        name="moe_down",
    )(route["st_e"], route["st_rows"], act, w_down, b_down.reshape(N_EXPERTS, 1, d))


def _combine_kernel(pos_ref, posn_ref, ys_hbm, x_ref, gate_ref, g_ref, op_ref, os_ref, buf, sem,
                    *, n_blk, n_prompt_blk):
    i = pl.program_id(0)
    slot = i % 2

    def row_copy(p, t, k, s):
        return pltpu.make_async_copy(ys_hbm.at[pl.ds(p, 1), :], buf.at[s, k, pl.ds(t, 1), :], sem.at[s])

    def issue(pos_smem, s):
        def body(t, c):
            for k in range(TOP_K):
                row_copy(pos_smem[t * TOP_K + k], t, k, s).start()
            return c
        lax.fori_loop(0, COMBINE_TOK, body, 0, unroll=DMA_UNROLL // TOP_K)

    @pl.when(i == 0)
    def _():
        issue(pos_ref, 0)

    @pl.when(i + 1 < n_blk)
    def _():
        issue(posn_ref, 1 - slot)

    def wait(t, c):
        for k in range(TOP_K):
            row_copy(0, t, k, slot).wait()
        return c

    lax.fori_loop(0, COMBINE_TOK, wait, 0, unroll=DMA_UNROLL // TOP_K)
    x = x_ref[...]
    for k in range(TOP_K):
        x = x + buf[slot, k] * gate_ref[:, k:k + 1]
    ms = jnp.mean(x * x, axis=-1, keepdims=True)
    y = x * lax.rsqrt(ms + NORM_EPS) * g_ref[...]

    @pl.when(i < n_prompt_blk)
    def _():
        op_ref[...] = y

    @pl.when(i >= n_prompt_blk)
    def _():
        os_ref[...] = y


def _combine(ys, pos, x, gates, g_final, rows_prompt):
    rows, d = x.shape
    n_blk = rows // COMBINE_TOK
    npb = rows_prompt // COMBINE_TOK
    nidx = COMBINE_TOK * TOP_K
    return pl.pallas_call(
        functools.partial(_combine_kernel, n_blk=n_blk, n_prompt_blk=npb),
        grid=(n_blk,),
        in_specs=[pl.BlockSpec((nidx,), lambda i: (i,), memory_space=pltpu.SMEM),
                  pl.BlockSpec((nidx,), lambda i: (jnp.minimum(i + 1, n_blk - 1),), memory_space=pltpu.SMEM),
                  pl.BlockSpec(memory_space=pl.ANY),
                  pl.BlockSpec((COMBINE_TOK, d), lambda i: (i, 0)),
                  pl.BlockSpec((COMBINE_TOK, LANES), lambda i: (i, 0)),
                  pl.BlockSpec((1, d), lambda i: (0, 0))],
        out_specs=[pl.BlockSpec((COMBINE_TOK, d), lambda i: (jnp.minimum(i, npb - 1), 0)),
                   pl.BlockSpec((COMBINE_TOK, d), lambda i: (jnp.maximum(i - npb, 0), 0))],
        out_shape=[jax.ShapeDtypeStruct((rows_prompt, d), F32),
                   jax.ShapeDtypeStruct((rows - rows_prompt, d), F32)],
        scratch_shapes=[pltpu.VMEM((2, TOP_K, COMBINE_TOK, d), F32), pltpu.SemaphoreType.DMA((2,))],
        compiler_params=_cparams(("arbitrary",), 48),
        name="moe_combine",
    )(pos, pos, ys, x, gates, g_final.reshape(1, d))


def _tail_rows(z, batch, seq, nrows, c0, c1):
    return jnp.stack([lax.slice(z, ((b + 1) * seq - nrows, c0), ((b + 1) * seq, c1)) for b in range(batch)])


def kernel(x_prompt, x_sample, cache_band_k, cache_band_v, state_conv, cache_mem_k, cache_mem_v, mem_prompt, norm_mix, w_in, rel_table, conv_w, conv_b, conv_ln_g, conv_ln_b, grp_g_att, grp_g_conv, w_out, norm_mem_q, norm_mem_kv, w_mem_q, w_mem_k, w_mem_v, w_mem_o, norm_ffn, w_router, b_router, w_gate_up, b_gate_up, w_down, b_down, norm_final):
    depth = norm_mix.shape[0]
    assert depth == 1
    bp, sp, d = x_prompt.shape
    bs, ss, _ = x_sample.shape
    tp, ts = bp * sp, bs * ss
    t_all = tp + ts
    assert d == D_MODEL and sp % QB == 0 and sp % CONV_TB == 0 and tp % MM_ROWS == 0
    assert sp >= BAND_WINDOW and ss == CHUNK and t_all % MM_ROWS == 0
    assert tp % COMBINE_TOK == 0 and ts % COMBINE_TOK == 0
    l = 0

    x = (x_prompt.reshape(tp, d), x_sample.reshape(ts, d))

    h = _rmsnorm(x, norm_mix[l], BF16)
    z = _matmul([h], w_in[l].astype(BF16), None, F32, 1024, "in_proj")
    w_rows = _rel_rows(rel_table[l])
    att_p = _attn_prompt(z, w_rows, grp_g_att[l], bp, sp)
    ck = cache_band_k[l].reshape(bs, -1, HEAD_DIM)
    cv = cache_band_v[l].reshape(bs, -1, HEAD_DIM)
    att_s = _attn_sample(z, tp, ck, cv, w_rows, grp_g_att[l], bs, ss)
    conv_params = (conv_w[l], conv_b[l].reshape(1, -1), conv_ln_g[l].reshape(1, -1),
                   conv_ln_b[l].reshape(1, -1), grp_g_conv[l].reshape(1, -1))
    cnv_p, cs_p = _conv_prompt(z, conv_params, bp, sp)
    past = jnp.pad(state_conv[l], ((0, 0), (CONV_HALO - (CONV_K - 1), 0), (0, 0)))
    cnv_s, cs_s = _conv_sample(z, tp, past, conv_params, bs, ss)
    x1 = _matmul([(att_p, att_s), (cnv_p, cnv_s)], w_out[l].astype(BF16), x, F32, 256, "out_proj")

    hd = (ATT_HEADS, HEAD_DIM)
    bk_p = _tail_rows(z, bp, sp, BAND_WINDOW, ATT_WIDTH, 2 * ATT_WIDTH).reshape(1, bp, BAND_WINDOW, *hd)
    bv_p = _tail_rows(z, bp, sp, BAND_WINDOW, 2 * ATT_WIDTH, 3 * ATT_WIDTH).reshape(1, bp, BAND_WINDOW, *hd)
    bk_s = lax.slice(z, (tp, ATT_WIDTH), (t_all, 2 * ATT_WIDTH)).reshape(1, bs, ss, *hd)
    bv_s = lax.slice(z, (tp, 2 * ATT_WIDTH), (t_all, 3 * ATT_WIDTH)).reshape(1, bs, ss, *hd)
    keep = CONV_HALO - (CONV_K - 1)
    cs_p = cs_p[:, keep:][None]
    cs_s = cs_s[:, keep:][None]

    nm = mem_prompt.shape[1]
    mem_n = _rmsnorm(mem_prompt.reshape(bp * nm, d), norm_mem_kv[l], BF16)
    w_mkv = jnp.concatenate([w_mem_k[l], w_mem_v[l]], axis=1).astype(BF16)
    mkv = _matmul([mem_n], w_mkv, None, F32, 2 * MEM_WIDTH, "mem_kv").reshape(bp, nm, 2 * MEM_WIDTH)
    mk_p = mkv[:, :, :MEM_WIDTH].reshape(1, bp, nm, MEM_HEADS, HEAD_DIM)
    mv_p = mkv[:, :, MEM_WIDTH:].reshape(1, bp, nm, MEM_HEADS, HEAD_DIM)
    hq = _rmsnorm(x1, norm_mem_q[l], BF16)
    q = _matmul([hq], w_mem_q[l].astype(BF16), None, F32, MEM_WIDTH, "mem_q")
    mo_p = _memattn(q, 0, mkv, mkv, 0, 1, bp, sp, 512, "memattn_prompt")
    cmk = cache_mem_k[l].reshape(bs, nm, MEM_WIDTH)
    cmv = cache_mem_v[l].reshape(bs, nm, MEM_WIDTH)
    mo_s = _memattn(q, tp, cmk, cmv, 0, 0, bs, ss, ss, "memattn_sample")
    x2 = _matmul([(mo_p, mo_s)], w_mem_o[l].astype(BF16), x1, F32, 512, "mem_o")

    hf = _rmsnorm(x2, norm_ffn[l], F32)
    top_idx, gates = _router(hf, w_router[l], b_router[l])
    route = _route(top_idx[:, :TOP_K])
    _, rows_tot = _moe_tiles(t_all * TOP_K)
    xs = _gather(hf, route, rows_tot)
    act = _moe_up(xs, route, w_gate_up[l], b_gate_up[l])
    ys = _moe_down(act, route, w_down[l], b_down[l])
    y_p, y_s = _combine(ys, route["pos"], x2, gates, norm_final, tp)

    return (y_p.reshape(bp, sp, d), y_s.reshape(bs, ss, d), bk_p, bv_p, cs_p, mk_p, mv_p, bk_s, bv_s, cs_s)
```
